```python
import math
import jax, jax.numpy as jnp
from jax import lax
import numpy as np

D_MODEL = 1024
BATCH = 8
SEQ = 2048
DEPTH = 4

GRID_W = 64
CTX_LEN = 256
N_MIXERS = 2
N_SUB = 3
D_FF = 2816
MACARON_W = 0.5
NORM_EPS = 1e-6
GDN_DK = 128
GDN_DV = 128
GDN_HEADS = D_MODEL // GDN_DV
GDN_CONV = 5
GDN_CHUNK = 64
GDN_IN = 4 * D_MODEL + 4 * GDN_HEADS
DIFF_HD = 64
DIFF_VD = 2 * DIFF_HD
DIFF_HEADS = D_MODEL // DIFF_VD
DIFF_IN = 3 * D_MODEL
Q_BLOCK = 128
ROPE_BASE = 10000.0
N_GDN = (DEPTH + 1) // 2
N_DIFF = DEPTH // 2

kernel_name = 'hybrid_gdn_diffattn_macaron_dit'

F32 = jnp.float32


def rmsnorm(x, gain):
    xf = x.astype(F32)
    y = xf * lax.rsqrt(jnp.mean(xf * xf, axis=-1, keepdims=True) + NORM_EPS)
    return (y * gain.astype(F32)).astype(x.dtype)


def l2norm(x):
    xf = x.astype(F32)
    return xf * lax.rsqrt(jnp.sum(xf * xf, axis=-1, keepdims=True) + NORM_EPS)


def modulation(cvec, w, b):
    m = jax.nn.silu(cvec) @ w + b
    return m.reshape(cvec.shape[0], N_SUB, 3, 1, D_MODEL).transpose(1, 2, 0, 3, 4)


def pre(x, mod_j, gain):
    shift, scale, _ = mod_j
    return rmsnorm(x, gain) * (1 + scale) + shift


def swiglu(h, w_in, w_out):
    g, u = jnp.split(h @ w_in, 2, axis=-1)
    return (jax.nn.silu(g) * u) @ w_out


def ffn_sublayer(x, mod_j, g_pre, g_post, w_in, w_out):
    h = pre(x, mod_j, g_pre)
    return x + MACARON_W * mod_j[2] * rmsnorm(swiglu(h, w_in, w_out), g_post)


def short_conv(x, w):
    ch = x.shape[-1]
    return lax.conv_general_dilated(x, w[:, None, :].astype(x.dtype), window_strides=(1,),
                                    padding=[(GDN_CONV // 2, GDN_CONV // 2)],
                                    dimension_numbers=('NWC', 'WIO', 'NWC'),
                                    feature_group_count=ch)


def gated_delta_scan(q, k, v, logdecay, beta, S0, with_out):
    B_, T, H, _ = q.shape
    C = GDN_CHUNK
    N = T // C

    def chunks(a):
        a = a.astype(F32).reshape((B_, N, C, H) + a.shape[3:])
        return jnp.moveaxis(a, (1, 3), (0, 2))

    qc, kc, vc, gc, bc = map(chunks, (q, k, v, logdecay, beta))
    G = jnp.cumsum(gc, axis=-1)
    idx = jnp.arange(C)
    incl = idx[:, None] >= idx[None, :]
    strict = idx[:, None] > idx[None, :]
    Gamma = jnp.exp(jnp.where(incl, G[..., :, None] - G[..., None, :], -jnp.inf))
    kb = kc * bc[..., None]
    A = jnp.where(strict, jnp.einsum('nbhcd,nbhsd->nbhcs', kb, kc) * Gamma, 0.0)
    eye = jnp.eye(C, dtype=F32)
    Tinv = lax.linalg.triangular_solve(eye + A, jnp.broadcast_to(eye, A.shape), left_side=True,
                                       lower=True, unit_diagonal=True)
    u = jnp.einsum('nbhcs,nbhse->nbhce', Tinv, vc * bc[..., None])
    w = jnp.einsum('nbhcs,nbhsd->nbhcd', Tinv, kb * jnp.exp(G)[..., None])
    G_last = G[..., -1]
    k_tail = kc * jnp.exp(G_last[..., None] - G)[..., None]
    xs = (u, w, k_tail, G_last)
    if with_out:
        q_dec = qc * jnp.exp(G)[..., None]
        intra = jnp.einsum('nbhcd,nbhsd->nbhcs', qc, kc) * Gamma
        xs = xs + (q_dec, intra)

    def step(S, xs_n):
        u_n, w_n, kt_n, gl_n = xs_n[:4]
        v_new = u_n - jnp.einsum('bhcd,bhde->bhce', w_n, S)
        S_next = S * jnp.exp(gl_n)[..., None, None] + jnp.einsum('bhcd,bhce->bhde', kt_n, v_new)
        if with_out:
            qd_n, in_n = xs_n[4:]
            o = jnp.einsum('bhcd,bhde->bhce', qd_n, S) + jnp.einsum('bhcs,bhse->bhce', in_n, v_new)
            return S_next, o
        return S_next, None

    S_fin, o = lax.scan(step, S0.astype(F32), xs)
    if with_out:
        o = jnp.moveaxis(o, (0, 2), (1, 3)).reshape(B_, T, H, v.shape[-1])
    return o, S_fin


def gdn_project(x, w_in, conv_w, A_log, dt_bias):
    B_, T = x.shape[:2]
    p = x @ w_in
    qkv, gate, a, b = jnp.split(p, [3 * D_MODEL, 4 * D_MODEL, 4 * D_MODEL + 2 * GDN_HEADS], axis=-1)
    qkv = jax.nn.silu(short_conv(qkv, conv_w))
    q, k, v = jnp.split(qkv, 3, axis=-1)
    q = l2norm(q.reshape(B_, T, GDN_HEADS, GDN_DK)) * (GDN_DK ** -0.5)
    k = l2norm(k.reshape(B_, T, GDN_HEADS, GDN_DK))
    v = v.reshape(B_, T, GDN_HEADS, GDN_DV)
    a = a.astype(F32).reshape(B_, T, 2, GDN_HEADS)
    b = b.astype(F32).reshape(B_, T, 2, GDN_HEADS)
    logdecay = -jnp.exp(A_log.astype(F32)) * jax.nn.softplus(a + dt_bias.astype(F32))
    beta = jax.nn.sigmoid(b)
    return q, k, v, gate, logdecay, beta


def gdn_out(o, gate, o_gain, w_out, dtype):
    B_, T = o.shape[:2]
    y = rmsnorm(o, o_gain) * jax.nn.silu(gate.astype(F32)).reshape(o.shape)
    return y.reshape(B_, T, D_MODEL).astype(dtype) @ w_out


def gdn_mixer(hl, hc, w_in, conv_w, A_log, dt_bias, o_gain, w_out, with_ctx_out):
    ql, kl, vl, gl, dl, bl = gdn_project(hl, w_in, conv_w, A_log, dt_bias)
    qc, kc, vc, gc, dc, bc = gdn_project(hc, w_in, conv_w, A_log, dt_bias)
    S0 = jnp.zeros((hl.shape[0], GDN_HEADS, GDN_DK, GDN_DV), F32)
    outs_l, outs_c = [], []
    for d in range(2):
        rev = (lambda a: a[:, ::-1]) if d == 1 else (lambda a: a)
        oc, Sc = gated_delta_scan(rev(qc), rev(kc), rev(vc), rev(dc[:, :, d]), rev(bc[:, :, d]), S0, with_ctx_out)
        ol, _ = gated_delta_scan(rev(ql), rev(kl), rev(vl), rev(dl[:, :, d]), rev(bl[:, :, d]), Sc, True)
        outs_l.append(rev(ol))
        if with_ctx_out:
            outs_c.append(rev(oc))
    yl = gdn_out(outs_l[0] + outs_l[1], gl, o_gain, w_out, hl.dtype)
    yc = gdn_out(outs_c[0] + outs_c[1], gc, o_gain, w_out, hc.dtype) if with_ctx_out else None
    return yl, yc


def axial_rope(row, col):
    half = DIFF_HD // 2
    inv = ROPE_BASE ** (-jnp.arange(0, half, 2, dtype=F32) / half)
    ang = jnp.concatenate([row[:, None].astype(F32) * inv, col[:, None].astype(F32) * inv], axis=-1)
    return jnp.cos(ang), jnp.sin(ang)


def apply_rope(x, cos, sin):
    xf = x.astype(F32)
    x1, x2 = xf[..., 0::2], xf[..., 1::2]
    c = cos[None, :, None, None, :]
    s = sin[None, :, None, None, :]
    return jnp.stack([x1 * c - x2 * s, x1 * s + x2 * c], axis=-1).reshape(x.shape).astype(x.dtype)


def diff_attend(qb, keys, vals, lam):
    s = jnp.einsum('bqhmd,bkhmd->bhmqk', qb, keys).astype(F32) * (DIFF_HD ** -0.5)
    p = jax.nn.softmax(s, axis=-1)
    a = p[:, :, 0] - lam * p[:, :, 1]
    return jnp.einsum('bhqk,bkhe->bqhe', a.astype(vals.dtype), vals)


def diff_out(o, subln, lam_init, w_out):
    B_, T = o.shape[:2]
    y = rmsnorm(o, subln) * (1.0 - lam_init)
    return y.reshape(B_, T, D_MODEL) @ w_out


def diff_mixer(hl, hc, w_in, lam_p, subln, w_out, lam_init, cos, sin, with_ctx_out):
    def project(x):
        B_, T = x.shape[:2]
        q, k, v = jnp.split(x @ w_in, 3, axis=-1)
        return (q.reshape(B_, T, DIFF_HEADS, 2, DIFF_HD), k.reshape(B_, T, DIFF_HEADS, 2, DIFF_HD),
                v.reshape(B_, T, DIFF_HEADS, DIFF_VD))
    ql, kl, vl = project(hl)
    qc, kc, vc = project(hc)
    ql, kl = apply_rope(ql, cos, sin), apply_rope(kl, cos, sin)
    lp = lam_p.astype(F32)
    lam = jnp.exp(jnp.sum(lp[0] * lp[1])) - jnp.exp(jnp.sum(lp[2] * lp[3])) + lam_init
    keys = jnp.concatenate([kc, kl], axis=1)
    vals = jnp.concatenate([vc, vl], axis=1)
    B_, T = hl.shape[:2]
    nb = T // Q_BLOCK
    qblocks = ql.reshape(B_, nb, Q_BLOCK, DIFF_HEADS, 2, DIFF_HD).transpose(1, 0, 2, 3, 4, 5)
    ol = lax.map(lambda qb: diff_attend(qb, keys, vals, lam), qblocks)
    ol = ol.transpose(1, 0, 2, 3, 4).reshape(B_, T, DIFF_HEADS, DIFF_VD)
    yl = diff_out(ol, subln, lam_init, w_out)
    yc = diff_out(diff_attend(qc, kc, vc, lam), subln, lam_init, w_out) if with_ctx_out else None
    return yl, yc


def setup_inputs(seed: int = 0) -> dict:
    key = jax.random.key(seed)
    ks = jax.random.split(key, 24)
    nrm = jax.random.normal
    D = D_MODEL
    dt = jnp.exp(jax.random.uniform(ks[12], (N_GDN, 2, GDN_HEADS), F32, math.log(1e-3), math.log(1e-1)))
    return {
        'x': nrm(ks[0], (BATCH, SEQ, D), F32),
        'c': nrm(ks[1], (BATCH, D), F32),
        'ctx': nrm(ks[2], (BATCH, CTX_LEN, D), F32),
        'c_ctx': nrm(ks[3], (D,), F32),
        'w_mod': nrm(ks[4], (DEPTH, D, N_SUB * 3 * D), F32) * (0.5 * D ** -0.5),
        'b_mod': nrm(ks[5], (DEPTH, N_SUB * 3 * D), F32) * 0.02,
        'norm_pre': 1.0 + 0.02 * nrm(ks[6], (DEPTH, N_SUB, D), F32),
        'norm_post': 1.0 + 0.02 * nrm(ks[7], (DEPTH, N_SUB, D), F32),
        'ffn_w_in': nrm(ks[8], (DEPTH, 2, D, 2 * D_FF), F32) * D ** -0.5,
        'ffn_w_out': nrm(ks[9], (DEPTH, 2, D_FF, D), F32) * D_FF ** -0.5,
        'gdn_w_in': nrm(ks[10], (N_GDN, D, GDN_IN), F32) * D ** -0.5,
        'gdn_conv': nrm(ks[11], (N_GDN, GDN_CONV, 3 * D), F32) * GDN_CONV ** -0.5,
        'gdn_A_log': jnp.log(jax.random.uniform(ks[13], (N_GDN, 2, GDN_HEADS), F32, 1.0, 16.0)),
        'gdn_dt_bias': dt + jnp.log(-jnp.expm1(-dt)),
        'gdn_o_gain': 1.0 + 0.02 * nrm(ks[14], (N_GDN, GDN_DV), F32),
        'gdn_w_out': nrm(ks[15], (N_GDN, D, D), F32) * D ** -0.5,
        'diff_w_in': nrm(ks[16], (N_DIFF, D, DIFF_IN), F32) * D ** -0.5,
        'diff_lambda': nrm(ks[17], (N_DIFF, 4, DIFF_HD), F32) * 0.1,
        'diff_subln': 1.0 + 0.02 * nrm(ks[18], (N_DIFF, DIFF_VD), F32),
        'diff_w_out': nrm(ks[19], (N_DIFF, D, D), F32) * D ** -0.5,
    }


def reference(x, c, ctx, c_ctx, w_mod, b_mod, norm_pre, norm_post, ffn_w_in, ffn_w_out,
              gdn_w_in, gdn_conv, gdn_A_log, gdn_dt_bias, gdn_o_gain, gdn_w_out,
              diff_w_in, diff_lambda, diff_subln, diff_w_out):
    T = x.shape[1]
    ROWS = T // GRID_W
    row = jnp.repeat(jnp.arange(ROWS), GRID_W)
    col = jnp.tile(jnp.arange(GRID_W), ROWS)
    cos, sin = axial_rope(row, col)
    xl, xc = x, ctx
    for l in range(DEPTH):
        last = l == DEPTH - 1
        mod_l = modulation(c, w_mod[l], b_mod[l])
        mod_c = modulation(c_ctx[None], w_mod[l], b_mod[l])
        xl = ffn_sublayer(xl, mod_l[0], norm_pre[l, 0], norm_post[l, 0], ffn_w_in[l, 0], ffn_w_out[l, 0])
        xc = ffn_sublayer(xc, mod_c[0], norm_pre[l, 0], norm_post[l, 0], ffn_w_in[l, 0], ffn_w_out[l, 0])
        hl = pre(xl, mod_l[1], norm_pre[l, 1])
        hc = pre(xc, mod_c[1], norm_pre[l, 1])
        i = l // N_MIXERS
        if l % N_MIXERS == 0:
            yl, yc = gdn_mixer(hl, hc, gdn_w_in[i], gdn_conv[i], gdn_A_log[i], gdn_dt_bias[i],
                               gdn_o_gain[i], gdn_w_out[i], not last)
        else:
            lam_init = 0.8 - 0.6 * math.exp(-0.3 * l)
            yl, yc = diff_mixer(hl, hc, diff_w_in[i], diff_lambda[i], diff_subln[i], diff_w_out[i],
                                lam_init, cos, sin, not last)
        xl = xl + mod_l[1][2] * rmsnorm(yl, norm_post[l, 1])
        if not last:
            xc = xc + mod_c[1][2] * rmsnorm(yc, norm_post[l, 1])
            xc = ffn_sublayer(xc, mod_c[2], norm_pre[l, 2], norm_post[l, 2], ffn_w_in[l, 1], ffn_w_out[l, 1])
        xl = ffn_sublayer(xl, mod_l[2], norm_pre[l, 2], norm_post[l, 2], ffn_w_in[l, 1], ffn_w_out[l, 1])
    return xl
```

```python
import functools
import math

import jax
import jax.numpy as jnp
from jax import lax
from jax.experimental import pallas as pl
from jax.experimental.pallas import tpu as pltpu

F32 = jnp.float32
BF16 = jnp.bfloat16

NORM_EPS = 1e-6
MACARON_W = 0.5
N_SUB = 3
GRID_W = 64
GDN_DK = 128
GDN_CONV = 5
DIFF_HD = 64
ROPE_BASE = 10000.0

LANES = 128
SUBLANES = 8
VMEM_LIMIT_BYTES = 56 * 1024 * 1024

ROW_TILE = 1024
FF_TILE = 256
PROJ_TILE = 512
GDN_CHUNK = 256
Q_TILE = 256
HALO = SUBLANES


def _pick_tile(n, target, align):
    best = None
    t = align
    while t <= min(n, target):
        if n % t == 0:
            best = t
        t += align
    assert best is not None, (n, target, align)
    return best


def _params(semantics):
    return pltpu.CompilerParams(dimension_semantics=semantics, vmem_limit_bytes=VMEM_LIMIT_BYTES)


def _dot(a, b):
    return jnp.dot(a, b, preferred_element_type=F32)


def _dot_nt(a, b):
    return lax.dot_general(a, b, (((1,), (1,)), ((), ())), preferred_element_type=F32)


def _dot_tn(a, b):
    return lax.dot_general(a, b, (((0,), (0,)), ((), ())), preferred_element_type=F32)


def _split3(x):
    x1 = x.astype(BF16)
    r1 = x - x1.astype(F32)
    x2 = r1.astype(BF16)
    x3 = (r1 - x2.astype(F32)).astype(BF16)
    return x1, x2, x3


def _rms(x, gain):
    return x * lax.rsqrt(jnp.mean(x * x, axis=-1, keepdims=True) + NORM_EPS) * gain


def _silu(x):
    return x * jax.nn.sigmoid(x)


def _pre(x, mod_ref, gain_ref):
    return _rms(x, gain_ref[...]) * (1.0 + mod_ref[1:2, :]) + mod_ref[0:1, :]


def _mod_kernel(c_ref, w_ref, b_ref, o_ref):
    c = c_ref[...]
    s = _silu(c).astype(BF16)
    o_ref[...] = _dot(s, w_ref[...].astype(BF16)) + b_ref[...]


def _modulation(cvec, w_mod, b_mod):
    depth, d, n = w_mod.shape
    rows = cvec.shape[0]
    tn = _pick_tile(n, 1152, LANES)
    return pl.pallas_call(
        _mod_kernel,
        grid=(depth, n // tn),
        in_specs=[
            pl.BlockSpec((rows, d), lambda l, j: (0, 0)),
            pl.BlockSpec((None, d, tn), lambda l, j: (l, 0, j)),
            pl.BlockSpec((None, 1, tn), lambda l, j: (l, 0, j)),
        ],
        out_specs=pl.BlockSpec((None, rows, tn), lambda l, j: (l, 0, j)),
        out_shape=jax.ShapeDtypeStruct((depth, rows, n), F32),
        compiler_params=_params(("parallel", "arbitrary")),
        name="modulation",
    )(cvec, w_mod, b_mod.reshape(depth, 1, n))


class _Geom:
    def __init__(self, batch, seq, ctx, d):
        self.batch, self.seq, self.ctx, self.d = batch, seq, ctx, d
        self.n_lat = batch * seq
        self.n_ctx = batch * ctx
        self.rows = self.n_lat + self.n_ctx
        self.tm = _pick_tile(math.gcd(seq, self.n_ctx), ROW_TILE, SUBLANES)
        self.lat_tiles = self.n_lat // self.tm
        self.all_tiles = self.rows // self.tm
        assert self.n_lat % ctx == 0

    def mod_row(self, i):
        return jnp.where(i < self.lat_tiles, (i * self.tm) // self.seq, self.batch)


def _ffn_kernel(x_ref, mod_ref, gpre_ref, gpost_ref, wg_ref, wu_ref, wo_ref, o_ref, h_ref, acc_ref):
    j = pl.program_id(1)

    @pl.when(j == 0)
    def _():
        h_ref[...] = _pre(x_ref[...], mod_ref, gpre_ref).astype(BF16)
        acc_ref[...] = jnp.zeros_like(acc_ref)

    h = h_ref[...]
    g = _dot(h, wg_ref[...])
    u = _dot(h, wu_ref[...])
    a = (_silu(g) * u).astype(BF16)
    acc_ref[...] += _dot(a, wo_ref[...])

    @pl.when(j == pl.num_programs(1) - 1)
    def _():
        r = _rms(acc_ref[...], gpost_ref[...])
        o_ref[...] = x_ref[...] + MACARON_W * mod_ref[2:3, :] * r


def _ffn(geom, x, mod, gpre, gpost, w_in, w_out, n_tiles):
    d, tm = geom.d, geom.tm
    dff = w_out.shape[0]
    tf = _pick_tile(dff, FF_TILE, LANES)
    nf = dff // tf
    return pl.pallas_call(
        _ffn_kernel,
        grid=(n_tiles, nf),
        in_specs=[
            pl.BlockSpec((tm, d), lambda i, j: (i, 0)),
            pl.BlockSpec((None, 3, d), lambda i, j: (geom.mod_row(i), 0, 0)),
            pl.BlockSpec((1, d), lambda i, j: (0, 0)),
            pl.BlockSpec((1, d), lambda i, j: (0, 0)),
            pl.BlockSpec((d, tf), lambda i, j: (0, j)),
            pl.BlockSpec((d, tf), lambda i, j: (0, nf + j)),
            pl.BlockSpec((tf, d), lambda i, j: (j, 0)),
        ],
        out_specs=pl.BlockSpec((tm, d), lambda i, j: (i, 0)),
        out_shape=jax.ShapeDtypeStruct((n_tiles * tm, d), F32),
        scratch_shapes=[pltpu.VMEM((tm, d), BF16), pltpu.VMEM((tm, d), F32)],
        compiler_params=_params(("parallel", "arbitrary")),
        name="ffn",
    )(x, mod, gpre, gpost, w_in, w_in, w_out)


def _outproj_kernel(*refs, lat_tiles, has_ctx):
    if has_ctx:
        x_ref, yl_ref, yc_ref, mod_ref, gpost_ref, w_ref, o_ref = refs
        y = jnp.where(pl.program_id(0) < lat_tiles, yl_ref[...], yc_ref[...])
    else:
        x_ref, yl_ref, mod_ref, gpost_ref, w_ref, o_ref = refs
        y = yl_ref[...]
    r = _rms(_dot(y, w_ref[...]), gpost_ref[...])
    o_ref[...] = x_ref[...] + mod_ref[2:3, :] * r


def _outproj(geom, x, yl, yc, mod, gpost, w):
    d, tm = geom.d, geom.tm
    has_ctx = yc is not None
    n_tiles = geom.all_tiles if has_ctx else geom.lat_tiles
    lt = geom.lat_tiles
    in_specs = [pl.BlockSpec((tm, d), lambda i: (i, 0)),
                pl.BlockSpec((tm, d), lambda i: (jnp.minimum(i, lt - 1), 0))]
    args = [x, yl]
    if has_ctx:
        in_specs.append(pl.BlockSpec((tm, d), lambda i: (jnp.maximum(i - lt, 0), 0)))
        args.append(yc)
    in_specs += [
        pl.BlockSpec((None, 3, d), lambda i: (geom.mod_row(i), 0, 0)),
        pl.BlockSpec((1, d), lambda i: (0, 0)),
        pl.BlockSpec((d, d), lambda i: (0, 0)),
    ]
    args += [mod, gpost, w]
    return pl.pallas_call(
        functools.partial(_outproj_kernel, lat_tiles=lt, has_ctx=has_ctx),
        grid=(n_tiles,),
        in_specs=in_specs,
        out_specs=pl.BlockSpec((tm, d), lambda i: (i, 0)),
        out_shape=jax.ShapeDtypeStruct((n_tiles * tm, d), F32),
        compiler_params=_params(("parallel",)),
        name="outproj",
    )(*args)


def _gdn_proj_kernel(x_ref, xp_ref, xn_ref, mod_ref, gpre_ref, w_ref, cw_ref, wab_ref, alog_ref, dtb_ref,
                     p_ref, gb_ref, h_ref, pbuf_ref, *, tm, seq, ctx, lat_tiles, n_qk, n_conv, heads):
    i = pl.program_id(0)
    j = pl.program_id(1)

    @pl.when(j == 0)
    def _():
        h_ref[0:HALO, :] = _pre(xp_ref[...], mod_ref, gpre_ref).astype(BF16)
        hm = _pre(x_ref[...], mod_ref, gpre_ref).astype(BF16)
        h_ref[HALO:HALO + tm, :] = hm
        h_ref[HALO + tm:, :] = _pre(xn_ref[...], mod_ref, gpre_ref).astype(BF16)
        ab = _dot(hm, wab_ref[...])
        lane = lax.broadcasted_iota(jnp.int32, ab.shape, 1)
        z = ab + dtb_ref[...]
        softplus = jnp.maximum(z, 0.0) + jnp.log1p(jnp.exp(-jnp.abs(z)))
        logdecay = -jnp.exp(alog_ref[...]) * softplus
        gb_ref[...] = jnp.where(lane < 2 * heads, logdecay, jax.nn.sigmoid(ab))

    p = _dot(h_ref[...], w_ref[...])

    @pl.when(j >= n_conv)
    def _():
        p_ref[...] = p[HALO:HALO + tm, :]

    @pl.when(j < n_conv)
    def _():
        pbuf_ref[...] = p
        seqlen = jnp.where(i < lat_tiles, seq, ctx)
        r = (i * tm) % seqlen + lax.broadcasted_iota(jnp.int32, (tm, 1), 0)
        wraps = jnp.floor((r.astype(F32) + 0.5) / seqlen.astype(F32)).astype(jnp.int32)
        pos = r - wraps * seqlen
        acc = jnp.zeros((tm, p.shape[1]), F32)
        for t in range(GDN_CONV):
            off = t - GDN_CONV // 2
            ok = jnp.logical_and(pos + off >= 0, pos + off < seqlen)
            tap = pbuf_ref[HALO + off:HALO + off + tm, :]
            acc = acc + jnp.where(ok, tap, 0.0) * cw_ref[t:t + 1, :]
        z = _silu(acc)

        @pl.when(j >= n_qk)
        def _():
            p_ref[...] = z

        @pl.when(j < n_qk)
        def _():
            scale = jnp.where(j < n_qk // 2, GDN_DK ** -0.5, 1.0)
            for c in range(z.shape[1] // GDN_DK):
                zc = z[:, c * GDN_DK:(c + 1) * GDN_DK]
                n = zc * lax.rsqrt(jnp.sum(zc * zc, axis=-1, keepdims=True) + NORM_EPS)
                p_ref[:, c * GDN_DK:(c + 1) * GDN_DK] = n * scale


def _gdn_proj(geom, x, mod, gpre, w_main, conv_w, w_ab, alog_row, dtb_row, heads):
    d, tm = geom.d, geom.tm
    n = w_main.shape[1]
    tn = _pick_tile(d, PROJ_TILE, GDN_DK)
    nj = n // tn
    n_conv = 3 * d // tn
    n_qk = 2 * d // tn
    hb = tm // HALO
    last_hb = geom.rows // HALO - 1
    kern = functools.partial(_gdn_proj_kernel, tm=tm, seq=geom.seq, ctx=geom.ctx, lat_tiles=geom.lat_tiles,
                             n_qk=n_qk, n_conv=n_conv, heads=heads)
    return pl.pallas_call(
        kern,
        grid=(geom.all_tiles, nj),
        in_specs=[
            pl.BlockSpec((tm, d), lambda i, j: (i, 0)),
            pl.BlockSpec((HALO, d), lambda i, j: (jnp.maximum(i * hb - 1, 0), 0)),
            pl.BlockSpec((HALO, d), lambda i, j: (jnp.minimum((i + 1) * hb, last_hb), 0)),
            pl.BlockSpec((None, 3, d), lambda i, j: (geom.mod_row(i), 0, 0)),
            pl.BlockSpec((1, d), lambda i, j: (0, 0)),
            pl.BlockSpec((d, tn), lambda i, j: (0, j)),
            pl.BlockSpec((GDN_CONV, tn), lambda i, j: (0, jnp.minimum(j, n_conv - 1))),
            pl.BlockSpec((d, LANES), lambda i, j: (0, 0)),
            pl.BlockSpec((1, LANES), lambda i, j: (0, 0)),
            pl.BlockSpec((1, LANES), lambda i, j: (0, 0)),
        ],
        out_specs=[
            pl.BlockSpec((tm, tn), lambda i, j: (i, j)),
            pl.BlockSpec((tm, LANES), lambda i, j: (i, 0)),
        ],
        out_shape=[
            jax.ShapeDtypeStruct((geom.rows, n), F32),
            jax.ShapeDtypeStruct((geom.rows, LANES), F32),
        ],
        scratch_shapes=[pltpu.VMEM((tm + 2 * HALO, d), BF16), pltpu.VMEM((tm + 2 * HALO, tn), F32)],
        compiler_params=_params(("parallel", "arbitrary")),
        name="gdn_proj",
    )(x, x, x, mod, gpre, w_main, conv_w, w_ab, alog_row, dtb_row)


def _gdn_chunk(q, k, v, gb, s, h, heads, reverse):
    c = q.shape[0]
    d = 1 if reverse else 0
    row = lax.broadcasted_iota(jnp.int32, (c, c), 0)
    col = lax.broadcasted_iota(jnp.int32, (c, c), 1)
    incl = (row <= col) if reverse else (row >= col)
    strict = (row < col) if reverse else (row > col)
    lane = lax.broadcasted_iota(jnp.int32, gb.shape, 1)

    def pick(idx):
        colv = jnp.sum(jnp.where(lane == idx, gb, 0.0), axis=1, keepdims=True)
        return jnp.broadcast_to(colv, gb.shape)

    g = pick(d * heads + h)
    beta = pick(2 * heads + d * heads + h)

    tri = jnp.where(incl, 1.0, 0.0).astype(BF16)
    g1, g2, g3 = _split3(g)
    gc = _dot(tri, g1) + _dot(tri, g2) + _dot(tri, g3)
    first_lane = jnp.where(lax.broadcasted_iota(jnp.int32, (SUBLANES, LANES), 1) == 0, 1.0, 0.0).astype(BF16)
    c1, c2, c3 = _split3(gc)
    grow = _dot_nt(first_lane, c1) + _dot_nt(first_lane, c2) + _dot_nt(first_lane, c3)
    grow = jnp.broadcast_to(grow[0:1, :], (c, c))
    gcol = jnp.concatenate([gc] * (c // LANES), axis=1)
    gam = jnp.where(incl, jnp.exp(jnp.where(incl, gcol - grow, 0.0)), 0.0)

    kb = k * beta
    k16 = k.astype(BF16)
    a = jnp.where(strict, _dot_nt(kb.astype(BF16), k16) * gam, 0.0)
    intra = _dot_nt(q.astype(BF16), k16) * gam
    eg = jnp.exp(gc)

    eye = jnp.where(row == col, 1.0, 0.0)
    leaf = SUBLANES.bit_length() - 1
    a8 = jnp.where((row >> leaf) == (col >> leaf), a, 0.0)
    a8sq = _dot(a8.astype(BF16), a8.astype(BF16))
    t = eye - a8
    t = t + _dot(t.astype(BF16), a8sq.astype(BF16))
    a8q = _dot(a8sq.astype(BF16), a8sq.astype(BF16))
    t = t + _dot(t.astype(BF16), a8q.astype(BF16))
    for lvl in range(leaf, c.bit_length() - 1):
        rb = row >> lvl
        cb = col >> lvl
        if reverse:
            off = jnp.logical_and(cb == rb + 1, (rb & 1) == 0)
        else:
            off = jnp.logical_and(rb == cb + 1, (rb & 1) == 1)
        t16 = t.astype(BF16)
        t = t - _dot(t16, _dot(jnp.where(off, a, 0.0).astype(BF16), t16).astype(BF16))

    r = jnp.concatenate([v * beta, kb * eg], axis=1)
    r = r + _dot((t - eye).astype(BF16), r.astype(BF16))
    u = r[:, :GDN_DK]
    w = r[:, GDN_DK:]

    g_last = gc[0:1, :] if reverse else gc[c - 1:c, :]
    k_tail = k * jnp.exp(g_last - gc)
    q_dec = q * eg
    s16 = s.astype(BF16)
    v_new = u - _dot(w.astype(BF16), s16)
    vn16 = v_new.astype(BF16)
    o = _dot(q_dec.astype(BF16), s16) + _dot(intra.astype(BF16), vn16)
    s_next = s * jnp.exp(g_last) + _dot_tn(k_tail.astype(BF16), vn16)
    return o, s_next


def _gdn_scan_kernel(ql, kl, vl, gl, gbl, qc, kc, vc, gc, gbc, og_ref, yl_ref, yc_ref, of_ref, ob_ref,
                     *, heads, chunk):
    h = pl.program_id(1)
    n_ctx = qc.shape[0]
    n_lat = ql.shape[0]
    cc = n_ctx // chunk
    lc = n_lat // chunk
    step = functools.partial(_gdn_chunk, h=h, heads=heads)
    s_f = jnp.zeros((GDN_DK, GDN_DK), F32)
    s_b = jnp.zeros((GDN_DK, GDN_DK), F32)

    for n in range(cc):
        rf = n * chunk
        rb = (cc - 1 - n) * chunk
        o, s_f = step(qc[rf:rf + chunk, :], kc[rf:rf + chunk, :], vc[rf:rf + chunk, :], gbc[rf:rf + chunk, :],
                      s_f, reverse=False)
        of_ref[rf:rf + chunk, :] = o
        o, s_b = step(qc[rb:rb + chunk, :], kc[rb:rb + chunk, :], vc[rb:rb + chunk, :], gbc[rb:rb + chunk, :],
                      s_b, reverse=True)
        ob_ref[rb:rb + chunk, :] = o

    def body(n, carry):
        s_f, s_b = carry
        rf = pl.multiple_of(n * chunk, chunk)
        rb = pl.multiple_of((lc - 1 - n) * chunk, chunk)
        o, s_f = step(ql[pl.ds(rf, chunk), :], kl[pl.ds(rf, chunk), :], vl[pl.ds(rf, chunk), :],
                      gbl[pl.ds(rf, chunk), :], s_f, reverse=False)
        of_ref[pl.ds(n_ctx + rf, chunk), :] = o
        o, s_b = step(ql[pl.ds(rb, chunk), :], kl[pl.ds(rb, chunk), :], vl[pl.ds(rb, chunk), :],
                      gbl[pl.ds(rb, chunk), :], s_b, reverse=True)
        ob_ref[pl.ds(n_ctx + rb, chunk), :] = o
        return s_f, s_b

    lax.fori_loop(0, lc, body, (s_f, s_b))

    og = og_ref[...]
    yc_ref[...] = (_rms(of_ref[0:n_ctx, :] + ob_ref[0:n_ctx, :], og) * _silu(gc[...])).astype(yc_ref.dtype)
    yl_ref[...] = (_rms(of_ref[n_ctx:, :] + ob_ref[n_ctx:, :], og) * _silu(gl[...])).astype(yl_ref.dtype)


def _gdn_scan(geom, p, gb, o_gain, heads):
    b, seq, ctx, d = geom.batch, geom.seq, geom.ctx, geom.d
    chunk = _pick_tile(math.gcd(seq, ctx), GDN_CHUNK, LANES)
    cb = geom.n_lat // ctx

    def lat(part):
        return pl.BlockSpec((seq, GDN_DK), lambda bi, hi: (bi, part * heads + hi))

    def cx(part):
        return pl.BlockSpec((ctx, GDN_DK), lambda bi, hi: (cb + bi, part * heads + hi))

    in_specs = [lat(0), lat(1), lat(2), lat(3), pl.BlockSpec((seq, LANES), lambda bi, hi: (bi, 0)),
                cx(0), cx(1), cx(2), cx(3), pl.BlockSpec((ctx, LANES), lambda bi, hi: (cb + bi, 0)),
                pl.BlockSpec((1, GDN_DK), lambda bi, hi: (0, 0))]
    return pl.pallas_call(
        functools.partial(_gdn_scan_kernel, heads=heads, chunk=chunk),
        grid=(b, heads),
        in_specs=in_specs,
        out_specs=[pl.BlockSpec((seq, GDN_DK), lambda bi, hi: (bi, hi)),
                   pl.BlockSpec((ctx, GDN_DK), lambda bi, hi: (bi, hi))],
        out_shape=[jax.ShapeDtypeStruct((geom.n_lat, d), BF16), jax.ShapeDtypeStruct((geom.n_ctx, d), BF16)],
        scratch_shapes=[pltpu.VMEM((ctx + seq, GDN_DK), F32), pltpu.VMEM((ctx + seq, GDN_DK), F32)],
        compiler_params=_params(("parallel", "parallel")),
        name="gdn_scan",
    )(p, p, p, p, gb, p, p, p, p, gb, o_gain)


def _diff_proj_kernel(x_ref, mod_ref, gpre_ref, w_ref, cos_ref, sin_ref, o_ref, h_ref, *, lat_tiles, n_q, n_qk):
    i = pl.program_id(0)
    j = pl.program_id(1)

    @pl.when(j == 0)
    def _():
        h_ref[...] = _pre(x_ref[...], mod_ref, gpre_ref).astype(BF16)

    p = _dot(h_ref[...], w_ref[...])
    scale = jnp.where(j < n_q, DIFF_HD ** -0.5, 1.0)
    rope = jnp.logical_and(i < lat_tiles, j < n_qk)

    @pl.when(rope)
    def _():
        tn = p.shape[1]
        reps = tn // LANES
        cos = jnp.concatenate([cos_ref[...]] * reps, axis=1)
        sin = jnp.concatenate([sin_ref[...]] * reps, axis=1)
        even = lax.broadcasted_iota(jnp.int32, p.shape, 1) % 2 == 0
        partner = jnp.where(even, pltpu.roll(p, tn - 1, 1), pltpu.roll(p, 1, 1))
        o_ref[...] = ((p * cos + partner * sin) * scale).astype(o_ref.dtype)

    @pl.when(jnp.logical_not(rope))
    def _():
        o_ref[...] = (p * scale).astype(o_ref.dtype)


def _diff_proj(geom, x, mod, gpre, w, cos_t, sin_t):
    d, tm = geom.d, geom.tm
    n = w.shape[1]
    tn = _pick_tile(d, PROJ_TILE, LANES)
    per_seq = geom.seq // tm
    kern = functools.partial(_diff_proj_kernel, lat_tiles=geom.lat_tiles, n_q=d // tn, n_qk=2 * d // tn)
    return pl.pallas_call(
        kern,
        grid=(geom.all_tiles, n // tn),
        in_specs=[
            pl.BlockSpec((tm, d), lambda i, j: (i, 0)),
            pl.BlockSpec((None, 3, d), lambda i, j: (geom.mod_row(i), 0, 0)),
            pl.BlockSpec((1, d), lambda i, j: (0, 0)),
            pl.BlockSpec((d, tn), lambda i, j: (0, j)),
            pl.BlockSpec((tm, LANES), lambda i, j: (i % per_seq, 0)),
            pl.BlockSpec((tm, LANES), lambda i, j: (i % per_seq, 0)),
        ],
        out_specs=pl.BlockSpec((tm, tn), lambda i, j: (i, j)),
        out_shape=jax.ShapeDtypeStruct((geom.rows, n), BF16),
        scratch_shapes=[pltpu.VMEM((tm, d), BF16)],
        compiler_params=_params(("parallel", "arbitrary")),
        name="diff_proj",
    )(x, mod, gpre, w, cos_t, sin_t)


def _diff_attn_kernel(*refs, n_seg, lam_init):
    q_ref = refs[0]
    k_refs = refs[1:1 + n_seg]
    v_refs = refs[1 + n_seg:1 + 2 * n_seg]
    lam_ref, subln_ref, o_ref = refs[1 + 2 * n_seg:]

    lp = lam_ref[...]
    lam = (jnp.exp(jnp.sum(lp[0:1, :] * lp[1:2, :], axis=1, keepdims=True))
           - jnp.exp(jnp.sum(lp[2:3, :] * lp[3:4, :], axis=1, keepdims=True)) + lam_init)

    q = q_ref[...]
    lane = lax.broadcasted_iota(jnp.int32, q.shape, 1)
    zero = jnp.zeros_like(q)
    comps = (jnp.where(lane < DIFF_HD, q, zero), jnp.where(lane >= DIFF_HD, q, zero))

    weights = []
    for m in range(2):
        s = [_dot_nt(comps[m], k_ref[...]) for k_ref in k_refs]
        mx = functools.reduce(jnp.maximum, [jnp.max(t, axis=1, keepdims=True) for t in s])
        e = [jnp.exp(t - mx) for t in s]
        z = functools.reduce(jnp.add, [jnp.sum(t, axis=1, keepdims=True) for t in e])
        weights.append((e, 1.0 / z))
    (e0, r0), (e1, r1) = weights
    r1 = lam * r1
    o = None
    for t in range(n_seg):
        a = (e0[t] * r0 - e1[t] * r1).astype(BF16)
        term = _dot(a, v_refs[t][...])
        o = term if o is None else o + term
    o_ref[...] = (_rms(o, subln_ref[...]) * (1.0 - lam_init)).astype(o_ref.dtype)


def _diff_attn(geom, qkv, lam_p, subln, lam_init, heads, context_queries):
    b, seq, ctx, d = geom.batch, geom.seq, geom.ctx, geom.d
    vd = 2 * DIFF_HD
    cb = geom.n_lat // ctx

    def lat(part):
        return pl.BlockSpec((seq, vd), lambda bi, hi, qi: (bi, part * heads + hi))

    def cx(part):
        return pl.BlockSpec((ctx, vd), lambda bi, hi, qi: (cb + bi, part * heads + hi))

    if context_queries:
        tq = _pick_tile(ctx, Q_TILE, SUBLANES)
        q0 = geom.n_lat // tq
        nq = ctx // tq
        k_specs, v_specs = [cx(1)], [cx(2)]
        out_rows = geom.n_ctx
    else:
        tq = _pick_tile(seq, Q_TILE, SUBLANES)
        q0 = 0
        nq = seq // tq
        k_specs, v_specs = [cx(1), lat(1)], [cx(2), lat(2)]
        out_rows = geom.n_lat
    n_seg = len(k_specs)
    in_specs = ([pl.BlockSpec((tq, vd), lambda bi, hi, qi: (q0 + bi * nq + qi, hi))] + k_specs + v_specs
                + [pl.BlockSpec(lam_p.shape, lambda bi, hi, qi: (0, 0)),
                   pl.BlockSpec((1, vd), lambda bi, hi, qi: (0, 0))])
    return pl.pallas_call(
        functools.partial(_diff_attn_kernel, n_seg=n_seg, lam_init=lam_init),
        grid=(b, heads, nq),
        in_specs=in_specs,
        out_specs=pl.BlockSpec((tq, vd), lambda bi, hi, qi: (bi * nq + qi, hi)),
        out_shape=jax.ShapeDtypeStruct((out_rows, d), BF16),
        compiler_params=_params(("parallel", "parallel", "arbitrary")),
        name="diff_attn_ctx" if context_queries else "diff_attn",
    )(*([qkv] * (1 + 2 * n_seg)), lam_p, subln)


def _rope_tables(seq):
    half = DIFF_HD // 2
    inv = ROPE_BASE ** (-jnp.arange(0, half, 2, dtype=F32) / half)
    pos = jnp.arange(seq)
    row = (pos // GRID_W).astype(F32)
    col = (pos % GRID_W).astype(F32)
    ang = jnp.concatenate([row[:, None] * inv, col[:, None] * inv], axis=-1)
    cos = jnp.repeat(jnp.cos(ang), 2, axis=-1)
    sin = jnp.repeat(jnp.sin(ang), 2, axis=-1) * jnp.tile(jnp.array([-1.0, 1.0], F32), half)
    return jnp.tile(cos, (1, 2)), jnp.tile(sin, (1, 2))


def kernel(x, c, ctx, c_ctx, w_mod, b_mod, norm_pre, norm_post, ffn_w_in, ffn_w_out, gdn_w_in, gdn_conv, gdn_A_log,
           gdn_dt_bias, gdn_o_gain, gdn_w_out, diff_w_in, diff_lambda, diff_subln, diff_w_out):
    batch, seq, d = x.shape
    n_ctx_tok = ctx.shape[1]
    depth = w_mod.shape[0]
    heads = d // GDN_DK
    geom = _Geom(batch, seq, n_ctx_tok, d)

    mod_rows = -(-(batch + 1) // SUBLANES) * SUBLANES
    cvec = jnp.concatenate([c, c_ctx[None], jnp.zeros((mod_rows - batch - 1, d), F32)], axis=0)
    mod = _modulation(cvec, w_mod, b_mod).reshape(depth, mod_rows, N_SUB, 3, d)

    cos_t, sin_t = _rope_tables(seq)
    xs = jnp.concatenate([x.reshape(batch * seq, d), ctx.reshape(batch * n_ctx_tok, d)], axis=0)

    for l in range(depth):
        last = l == depth - 1
        i = l // 2
        gpre = norm_pre[l].reshape(N_SUB, 1, d)
        gpost = norm_post[l].reshape(N_SUB, 1, d)
        w_in16 = ffn_w_in[l].astype(BF16)
        w_out16 = ffn_w_out[l].astype(BF16)

        xs = _ffn(geom, xs, mod[l, :, 0], gpre[0], gpost[0], w_in16[0], w_out16[0], geom.all_tiles)

        if l % 2 == 0:
            w = gdn_w_in[i]
            w_ab = jnp.pad(w[:, 4 * d:], ((0, 0), (0, LANES - 4 * heads))).astype(BF16)
            pad = (0, LANES - 2 * heads)
            alog_row = jnp.pad(gdn_A_log[i].reshape(-1), pad).reshape(1, LANES)
            dtb_row = jnp.pad(gdn_dt_bias[i].reshape(-1), pad).reshape(1, LANES)
            p, gb = _gdn_proj(geom, xs, mod[l, :, 1], gpre[1], w[:, :4 * d].astype(BF16), gdn_conv[i], w_ab,
                              alog_row, dtb_row, heads)
            yl, yc = _gdn_scan(geom, p, gb, gdn_o_gain[i].reshape(1, GDN_DK), heads)
            w_o = gdn_w_out[i].astype(BF16)
        else:
            lam_init = 0.8 - 0.6 * math.exp(-0.3 * l)
            qkv = _diff_proj(geom, xs, mod[l, :, 1], gpre[1], diff_w_in[i].astype(BF16), cos_t, sin_t)
            subln = diff_subln[i].reshape(1, 2 * DIFF_HD)
            yl = _diff_attn(geom, qkv, diff_lambda[i], subln, lam_init, heads, context_queries=False)
            yc = None if last else _diff_attn(geom, qkv, diff_lambda[i], subln, lam_init, heads, context_queries=True)
            w_o = diff_w_out[i].astype(BF16)

        xs = _outproj(geom, xs, yl, None if last else yc, mod[l, :, 1], gpost[1], w_o)
        xs = _ffn(geom, xs, mod[l, :, 2], gpre[2], gpost[2], w_in16[1], w_out16[1],
                  geom.lat_tiles if last else geom.all_tiles)

    return xs.reshape(batch, seq, d)
```

```python
import functools
import math

import jax
import jax.numpy as jnp
from jax import lax
from jax.experimental import pallas as pl
from jax.experimental.pallas import tpu as pltpu

F32 = jnp.float32
BF16 = jnp.bfloat16

NORM_EPS = 1e-6
MACARON_W = 0.5
N_SUB = 3
GRID_W = 64
GDN_DK = 128
GDN_CONV = 5
DIFF_HD = 64
ROPE_BASE = 10000.0

LANES = 128
SUBLANES = 8
VMEM_LIMIT_BYTES = 56 * 1024 * 1024

ROW_TILE = 1024
FF_TILE = 256
PROJ_TILE = 512
GDN_CHUNK = 256
GDN_PREP_GROUP = 3
Q_TILE = 256
HALO = SUBLANES


def _pick_tile(n, target, align):
    best = None
    t = align
    while t <= min(n, target):
        if n % t == 0:
            best = t
        t += align
    assert best is not None, (n, target, align)
    return best


def _params(semantics):
    return pltpu.CompilerParams(dimension_semantics=semantics, vmem_limit_bytes=VMEM_LIMIT_BYTES)


def _dot(a, b):
    return jnp.dot(a, b, preferred_element_type=F32)


def _dot_nt(a, b):
    return lax.dot_general(a, b, (((1,), (1,)), ((), ())), preferred_element_type=F32)


def _dot_tn(a, b):
    return lax.dot_general(a, b, (((0,), (0,)), ((), ())), preferred_element_type=F32)


def _split3(x):
    x1 = x.astype(BF16)
    r1 = x - x1.astype(F32)
    x2 = r1.astype(BF16)
    x3 = (r1 - x2.astype(F32)).astype(BF16)
    return x1, x2, x3


def _rms(x, gain):
    return x * lax.rsqrt(jnp.mean(x * x, axis=-1, keepdims=True) + NORM_EPS) * gain


def _silu(x):
    return x * jax.nn.sigmoid(x)


def _pre(x, mod_ref, gain_ref):
    return _rms(x, gain_ref[...]) * (1.0 + mod_ref[1:2, :]) + mod_ref[0:1, :]


def _mod_kernel(c_ref, w_ref, b_ref, o_ref):
    c = c_ref[...]
    s = _silu(c).astype(BF16)
    o_ref[...] = _dot(s, w_ref[...].astype(BF16)) + b_ref[...]


def _modulation(cvec, w_mod, b_mod):
    depth, d, n = w_mod.shape
    rows = cvec.shape[0]
    tn = _pick_tile(n, 1152, LANES)
    return pl.pallas_call(
        _mod_kernel,
        grid=(depth, n // tn),
        in_specs=[
            pl.BlockSpec((rows, d), lambda l, j: (0, 0)),
            pl.BlockSpec((None, d, tn), lambda l, j: (l, 0, j)),
            pl.BlockSpec((None, 1, tn), lambda l, j: (l, 0, j)),
        ],
        out_specs=pl.BlockSpec((None, rows, tn), lambda l, j: (l, 0, j)),
        out_shape=jax.ShapeDtypeStruct((depth, rows, n), F32),
        compiler_params=_params(("parallel", "arbitrary")),
        name="modulation",
    )(cvec, w_mod, b_mod.reshape(depth, 1, n))


class _Geom:
    def __init__(self, batch, seq, ctx, d):
        self.batch, self.seq, self.ctx, self.d = batch, seq, ctx, d
        self.n_lat = batch * seq
        self.n_ctx = batch * ctx
        self.rows = self.n_lat + self.n_ctx
        self.tm = _pick_tile(math.gcd(seq, self.n_ctx), ROW_TILE, SUBLANES)
        self.lat_tiles = self.n_lat // self.tm
        self.all_tiles = self.rows // self.tm
        assert self.n_lat % ctx == 0

    def mod_row(self, i):
        return jnp.where(i < self.lat_tiles, (i * self.tm) // self.seq, self.batch)


def _ffn_kernel(x_ref, mod_ref, gpre_ref, gpost_ref, wg_ref, wu_ref, wo_ref, o_ref, h_ref, acc_ref):
    j = pl.program_id(1)

    @pl.when(j == 0)
    def _():
        h_ref[...] = _pre(x_ref[...], mod_ref, gpre_ref).astype(BF16)
        acc_ref[...] = jnp.zeros_like(acc_ref)

    h = h_ref[...]
    g = _dot(h, wg_ref[...])
    u = _dot(h, wu_ref[...])
    a = (_silu(g) * u).astype(BF16)
    acc_ref[...] += _dot(a, wo_ref[...])

    @pl.when(j == pl.num_programs(1) - 1)
    def _():
        r = _rms(acc_ref[...], gpost_ref[...])
        o_ref[...] = x_ref[...] + MACARON_W * mod_ref[2:3, :] * r


def _ffn(geom, x, mod, gpre, gpost, w_in, w_out, n_tiles):
    d, tm = geom.d, geom.tm
    dff = w_out.shape[0]
    tf = _pick_tile(dff, FF_TILE, LANES)
    nf = dff // tf
    return pl.pallas_call(
        _ffn_kernel,
        grid=(n_tiles, nf),
        in_specs=[
            pl.BlockSpec((tm, d), lambda i, j: (i, 0)),
            pl.BlockSpec((None, 3, d), lambda i, j: (geom.mod_row(i), 0, 0)),
            pl.BlockSpec((1, d), lambda i, j: (0, 0)),
            pl.BlockSpec((1, d), lambda i, j: (0, 0)),
            pl.BlockSpec((d, tf), lambda i, j: (0, j)),
            pl.BlockSpec((d, tf), lambda i, j: (0, nf + j)),
            pl.BlockSpec((tf, d), lambda i, j: (j, 0)),
        ],
        out_specs=pl.BlockSpec((tm, d), lambda i, j: (i, 0)),
        out_shape=jax.ShapeDtypeStruct((n_tiles * tm, d), F32),
        scratch_shapes=[pltpu.VMEM((tm, d), BF16), pltpu.VMEM((tm, d), F32)],
        compiler_params=_params(("parallel", "arbitrary")),
        name="ffn",
    )(x, mod, gpre, gpost, w_in, w_in, w_out)


def _outproj_kernel(*refs, lat_tiles, has_ctx):
    if has_ctx:
        x_ref, yl_ref, yc_ref, mod_ref, gpost_ref, w_ref, o_ref = refs
        y = jnp.where(pl.program_id(0) < lat_tiles, yl_ref[...], yc_ref[...])
    else:
        x_ref, yl_ref, mod_ref, gpost_ref, w_ref, o_ref = refs
        y = yl_ref[...]
    r = _rms(_dot(y, w_ref[...]), gpost_ref[...])
    o_ref[...] = x_ref[...] + mod_ref[2:3, :] * r


def _outproj(geom, x, yl, yc, mod, gpost, w):
    d, tm = geom.d, geom.tm
    has_ctx = yc is not None
    n_tiles = geom.all_tiles if has_ctx else geom.lat_tiles
    lt = geom.lat_tiles
    in_specs = [pl.BlockSpec((tm, d), lambda i: (i, 0)),
                pl.BlockSpec((tm, d), lambda i: (jnp.minimum(i, lt - 1), 0))]
    args = [x, yl]
    if has_ctx:
        in_specs.append(pl.BlockSpec((tm, d), lambda i: (jnp.maximum(i - lt, 0), 0)))
        args.append(yc)
    in_specs += [
        pl.BlockSpec((None, 3, d), lambda i: (geom.mod_row(i), 0, 0)),
        pl.BlockSpec((1, d), lambda i: (0, 0)),
        pl.BlockSpec((d, d), lambda i: (0, 0)),
    ]
    args += [mod, gpost, w]
    return pl.pallas_call(
        functools.partial(_outproj_kernel, lat_tiles=lt, has_ctx=has_ctx),
        grid=(n_tiles,),
        in_specs=in_specs,
        out_specs=pl.BlockSpec((tm, d), lambda i: (i, 0)),
        out_shape=jax.ShapeDtypeStruct((n_tiles * tm, d), F32),
        compiler_params=_params(("parallel",)),
        name="outproj",
    )(*args)


def _gdn_proj_kernel(x_ref, xp_ref, xn_ref, mod_ref, gpre_ref, w_ref, cw_ref, wab_ref, alog_ref, dtb_ref,
                     p_ref, gb_ref, h_ref, pbuf_ref, *, tm, seq, ctx, lat_tiles, n_qk, n_conv, heads):
    i = pl.program_id(0)
    j = pl.program_id(1)

    @pl.when(j == 0)
    def _():
        h_ref[0:HALO, :] = _pre(xp_ref[...], mod_ref, gpre_ref).astype(BF16)
        hm = _pre(x_ref[...], mod_ref, gpre_ref).astype(BF16)
        h_ref[HALO:HALO + tm, :] = hm
        h_ref[HALO + tm:, :] = _pre(xn_ref[...], mod_ref, gpre_ref).astype(BF16)
        ab = _dot(hm, wab_ref[...])
        lane = lax.broadcasted_iota(jnp.int32, ab.shape, 1)
        z = ab + dtb_ref[...]
        softplus = jnp.maximum(z, 0.0) + jnp.log1p(jnp.exp(-jnp.abs(z)))
        logdecay = -jnp.exp(alog_ref[...]) * softplus
        gb_ref[...] = jnp.where(lane < 2 * heads, logdecay, jax.nn.sigmoid(ab))

    p = _dot(h_ref[...], w_ref[...])

    @pl.when(j >= n_conv)
    def _():
        p_ref[...] = p[HALO:HALO + tm, :]

    @pl.when(j < n_conv)
    def _():
        pbuf_ref[...] = p
        seqlen = jnp.where(i < lat_tiles, seq, ctx)
        r = (i * tm) % seqlen + lax.broadcasted_iota(jnp.int32, (tm, 1), 0)
        wraps = jnp.floor((r.astype(F32) + 0.5) / seqlen.astype(F32)).astype(jnp.int32)
        pos = r - wraps * seqlen
        acc = jnp.zeros((tm, p.shape[1]), F32)
        for t in range(GDN_CONV):
            off = t - GDN_CONV // 2
            ok = jnp.logical_and(pos + off >= 0, pos + off < seqlen)
            tap = pbuf_ref[HALO + off:HALO + off + tm, :]
            acc = acc + jnp.where(ok, tap, 0.0) * cw_ref[t:t + 1, :]
        z = _silu(acc)

        @pl.when(j >= n_qk)
        def _():
            p_ref[...] = z

        @pl.when(j < n_qk)
        def _():
            scale = jnp.where(j < n_qk // 2, GDN_DK ** -0.5, 1.0)
            for c in range(z.shape[1] // GDN_DK):
                zc = z[:, c * GDN_DK:(c + 1) * GDN_DK]
                n = zc * lax.rsqrt(jnp.sum(zc * zc, axis=-1, keepdims=True) + NORM_EPS)
                p_ref[:, c * GDN_DK:(c + 1) * GDN_DK] = n * scale


def _gdn_proj(geom, x, mod, gpre, w_main, conv_w, w_ab, alog_row, dtb_row, heads):
    d, tm = geom.d, geom.tm
    n = w_main.shape[1]
    tn = _pick_tile(d, PROJ_TILE, GDN_DK)
    nj = n // tn
    n_conv = 3 * d // tn
    n_qk = 2 * d // tn
    hb = tm // HALO
    last_hb = geom.rows // HALO - 1
    kern = functools.partial(_gdn_proj_kernel, tm=tm, seq=geom.seq, ctx=geom.ctx, lat_tiles=geom.lat_tiles,
                             n_qk=n_qk, n_conv=n_conv, heads=heads)
    return pl.pallas_call(
        kern,
        grid=(geom.all_tiles, nj),
        in_specs=[
            pl.BlockSpec((tm, d), lambda i, j: (i, 0)),
            pl.BlockSpec((HALO, d), lambda i, j: (jnp.maximum(i * hb - 1, 0), 0)),
            pl.BlockSpec((HALO, d), lambda i, j: (jnp.minimum((i + 1) * hb, last_hb), 0)),
            pl.BlockSpec((None, 3, d), lambda i, j: (geom.mod_row(i), 0, 0)),
            pl.BlockSpec((1, d), lambda i, j: (0, 0)),
            pl.BlockSpec((d, tn), lambda i, j: (0, j)),
            pl.BlockSpec((GDN_CONV, tn), lambda i, j: (0, jnp.minimum(j, n_conv - 1))),
            pl.BlockSpec((d, LANES), lambda i, j: (0, 0)),
            pl.BlockSpec((1, LANES), lambda i, j: (0, 0)),
            pl.BlockSpec((1, LANES), lambda i, j: (0, 0)),
        ],
        out_specs=[
            pl.BlockSpec((tm, tn), lambda i, j: (i, j)),
            pl.BlockSpec((tm, LANES), lambda i, j: (i, 0)),
        ],
        out_shape=[
            jax.ShapeDtypeStruct((geom.rows, n), F32),
            jax.ShapeDtypeStruct((geom.rows, LANES), F32),
        ],
        scratch_shapes=[pltpu.VMEM((tm + 2 * HALO, d), BF16), pltpu.VMEM((tm + 2 * HALO, tn), F32)],
        compiler_params=_params(("parallel", "arbitrary")),
        name="gdn_proj",
    )(x, x, x, mod, gpre, w_main, conv_w, w_ab, alog_row, dtb_row)


def _gdn_prep(chains, h, heads):
    n = len(chains)
    qs, ks, vs, gbs, revs = zip(*chains)
    c = qs[0].shape[0]
    row = lax.broadcasted_iota(jnp.int32, (c, c), 0)
    col = lax.broadcasted_iota(jnp.int32, (c, c), 1)
    lane = lax.broadcasted_iota(jnp.int32, gbs[0].shape, 1)
    eye = jnp.where(row == col, 1.0, 0.0)
    incl = {False: row >= col, True: row <= col}
    strict = {False: row > col, True: row < col}
    tri = {r: jnp.where(m, 1.0, 0.0).astype(BF16) for r, m in incl.items()}
    first_lane = jnp.where(lax.broadcasted_iota(jnp.int32, (SUBLANES, LANES), 1) == 0, 1.0, 0.0).astype(BF16)
    rng = range(n)

    def pick(gb, idx):
        colv = jnp.sum(jnp.where(lane == idx, gb, 0.0), axis=1, keepdims=True)
        return jnp.broadcast_to(colv, gb.shape)

    g = [pick(gbs[i], int(revs[i]) * heads + h) for i in rng]
    beta = [pick(gbs[i], 2 * heads + int(revs[i]) * heads + h) for i in rng]

    gp = [_split3(x) for x in g]
    gc = [sum(_dot(tri[revs[i]], p) for p in gp[i]) for i in rng]
    cp = [_split3(x) for x in gc]
    grow = [sum(_dot_nt(first_lane, p) for p in cp[i]) for i in rng]
    gam = []
    for i in rng:
        gr = jnp.broadcast_to(grow[i][0:1, :], (c, c))
        gcol = jnp.concatenate([gc[i]] * (c // LANES), axis=1)
        m = incl[revs[i]]
        gam.append(jnp.where(m, jnp.exp(jnp.where(m, gcol - gr, 0.0)), 0.0))

    kb = [ks[i] * beta[i] for i in rng]
    k16 = [x.astype(BF16) for x in ks]
    kk = [_dot_nt(kb[i].astype(BF16), k16[i]) for i in rng]
    qk = [_dot_nt(qs[i].astype(BF16), k16[i]) for i in rng]
    a = [jnp.where(strict[revs[i]], kk[i] * gam[i], 0.0) for i in rng]
    intra = [(qk[i] * gam[i]).astype(BF16) for i in rng]
    eg = [jnp.exp(x) for x in gc]

    leaf = SUBLANES.bit_length() - 1
    in_leaf = (row >> leaf) == (col >> leaf)
    a8 = [jnp.where(in_leaf, x, 0.0) for x in a]
    a8b = [x.astype(BF16) for x in a8]
    sq = [_dot(x, x).astype(BF16) for x in a8b]
    t = [eye - x for x in a8]
    t = [t[i] + _dot(t[i].astype(BF16), sq[i]) for i in rng]
    qd = [_dot(x, x).astype(BF16) for x in sq]
    t = [t[i] + _dot(t[i].astype(BF16), qd[i]) for i in rng]
    for lvl in range(leaf, c.bit_length() - 1):
        rb = row >> lvl
        cb = col >> lvl
        off = {False: jnp.logical_and(rb == cb + 1, (rb & 1) == 1),
               True: jnp.logical_and(cb == rb + 1, (rb & 1) == 0)}
        t16 = [x.astype(BF16) for x in t]
        inner = [_dot(jnp.where(off[revs[i]], a[i], 0.0).astype(BF16), t16[i]).astype(BF16) for i in rng]
        t = [t[i] - _dot(t16[i], inner[i]) for i in rng]

    r = [jnp.concatenate([vs[i] * beta[i], kb[i] * eg[i]], axis=1) for i in rng]
    r = [r[i] + _dot((t[i] - eye).astype(BF16), r[i].astype(BF16)) for i in rng]

    out = []
    for i in rng:
        g_last = gc[i][0:1, :] if revs[i] else gc[i][c - 1:c, :]
        k_tail = ks[i] * jnp.exp(g_last - gc[i])
        out.append((r[i][:, :GDN_DK], r[i][:, GDN_DK:].astype(BF16), (qs[i] * eg[i]).astype(BF16),
                    k_tail.astype(BF16), intra[i], jnp.exp(g_last)))
    return out


def _gdn_step(u, w, q_dec, k_tail, intra, decay, s):
    s16 = s.astype(BF16)
    vn16 = (u - _dot(w, s16)).astype(BF16)
    o = _dot(q_dec, s16) + _dot(intra, vn16)
    return o, s * decay + _dot_tn(k_tail, vn16)


def _gdn_scan_kernel(ql, kl, vl, gl, gbl, qc, kc, vc, gc, gbc, og_ref, yl_ref, yc_ref,
                     q_s, k_s, v_s, gb_s, u_s, w_s, qd_s, kt_s, in_s, dec_s, o_s, *, heads, chunk, group):
    h = pl.program_id(1)
    n_ctx = qc.shape[0]
    n_tok = q_s.shape[0]
    cc = n_ctx // chunk
    nc = n_tok // chunk

    for dst, c_ref, l_ref in ((q_s, qc, ql), (k_s, kc, kl), (v_s, vc, vl), (gb_s, gbc, gbl)):
        dst[0:n_ctx, :] = c_ref[...]
        dst[n_ctx:, :] = l_ref[...]

    def prep_body(it, carry):
        chains, where = [], []
        for j in range(group):
            ci = it * group + j
            rows = pl.ds(pl.multiple_of(ci * chunk, chunk), chunk)
            for d in range(2):
                chains.append((q_s[rows, :], k_s[rows, :], v_s[rows, :], gb_s[rows, :], bool(d)))
                where.append((d, rows, pl.ds(pl.multiple_of(ci * SUBLANES, SUBLANES), SUBLANES)))
        for (d, rows, dec_rows), (u, w, qd, kt, intra, dec) in zip(where, _gdn_prep(chains, h, heads)):
            u_s[d, rows, :] = u
            w_s[d, rows, :] = w
            qd_s[d, rows, :] = qd
            kt_s[d, rows, :] = kt
            in_s[d, rows, :] = intra
            dec_s[d, dec_rows, :] = jnp.broadcast_to(dec, (SUBLANES, GDN_DK))
        return carry

    lax.fori_loop(0, nc // group, prep_body, 0)

    def scan_body(n, carry):
        out = []
        for d in range(2):
            ci = n if d == 0 else jnp.where(n < cc, cc - 1 - n, nc - 1 - (n - cc))
            rows = pl.ds(pl.multiple_of(ci * chunk, chunk), chunk)
            dec = dec_s[d, pl.ds(pl.multiple_of(ci * SUBLANES, SUBLANES), SUBLANES), :][0:1, :]
            o, s_next = _gdn_step(u_s[d, rows, :], w_s[d, rows, :], qd_s[d, rows, :], kt_s[d, rows, :],
                                  in_s[d, rows, :], dec, carry[d])
            o_s[d, rows, :] = o
            out.append(s_next)
        return tuple(out)

    zero = jnp.zeros((GDN_DK, GDN_DK), F32)
    lax.fori_loop(0, nc, scan_body, (zero, zero))

    og = og_ref[...]
    yc_ref[...] = (_rms(o_s[0, 0:n_ctx, :] + o_s[1, 0:n_ctx, :], og) * _silu(gc[...])).astype(yc_ref.dtype)
    yl_ref[...] = (_rms(o_s[0, n_ctx:, :] + o_s[1, n_ctx:, :], og) * _silu(gl[...])).astype(yl_ref.dtype)


def _gdn_scan(geom, p, gb, o_gain, heads):
    b, seq, ctx, d = geom.batch, geom.seq, geom.ctx, geom.d
    chunk = _pick_tile(math.gcd(seq, ctx), GDN_CHUNK, LANES)
    n_tok = ctx + seq
    nc = n_tok // chunk
    group = max(g for g in range(1, GDN_PREP_GROUP + 1) if nc % g == 0)
    cb = geom.n_lat // ctx

    def lat(part):
        return pl.BlockSpec((seq, GDN_DK), lambda bi, hi: (bi, part * heads + hi))

    def cx(part):
        return pl.BlockSpec((ctx, GDN_DK), lambda bi, hi: (cb + bi, part * heads + hi))

    in_specs = [lat(0), lat(1), lat(2), lat(3), pl.BlockSpec((seq, LANES), lambda bi, hi: (bi, 0)),
                cx(0), cx(1), cx(2), cx(3), pl.BlockSpec((ctx, LANES), lambda bi, hi: (cb + bi, 0)),
                pl.BlockSpec((1, GDN_DK), lambda bi, hi: (0, 0))]
    tok = lambda dt: pltpu.VMEM((n_tok, GDN_DK), dt)
    per_dir = lambda cols, dt: pltpu.VMEM((2, n_tok, cols), dt)
    return pl.pallas_call(
        functools.partial(_gdn_scan_kernel, heads=heads, chunk=chunk, group=group),
        grid=(b, heads),
        in_specs=in_specs,
        out_specs=[pl.BlockSpec((seq, GDN_DK), lambda bi, hi: (bi, hi)),
                   pl.BlockSpec((ctx, GDN_DK), lambda bi, hi: (bi, hi))],
        out_shape=[jax.ShapeDtypeStruct((geom.n_lat, d), BF16), jax.ShapeDtypeStruct((geom.n_ctx, d), BF16)],
        scratch_shapes=[tok(F32), tok(F32), tok(F32), tok(F32),
                        per_dir(GDN_DK, F32), per_dir(GDN_DK, BF16), per_dir(GDN_DK, BF16),
                        per_dir(GDN_DK, BF16), per_dir(chunk, BF16),
                        pltpu.VMEM((2, nc * SUBLANES, GDN_DK), F32), per_dir(GDN_DK, F32)],
        compiler_params=_params(("parallel", "parallel")),
        name="gdn_scan",
    )(p, p, p, p, gb, p, p, p, p, gb, o_gain)


def _diff_proj_kernel(x_ref, mod_ref, gpre_ref, w_ref, cos_ref, sin_ref, o_ref, h_ref, *, lat_tiles, n_q, n_qk):
    i = pl.program_id(0)
    j = pl.program_id(1)

    @pl.when(j == 0)
    def _():
        h_ref[...] = _pre(x_ref[...], mod_ref, gpre_ref).astype(BF16)

    p = _dot(h_ref[...], w_ref[...])
    scale = jnp.where(j < n_q, DIFF_HD ** -0.5, 1.0)
    rope = jnp.logical_and(i < lat_tiles, j < n_qk)

    @pl.when(rope)
    def _():
        tn = p.shape[1]
        reps = tn // LANES
        cos = jnp.concatenate([cos_ref[...]] * reps, axis=1)
        sin = jnp.concatenate([sin_ref[...]] * reps, axis=1)
        even = lax.broadcasted_iota(jnp.int32, p.shape, 1) % 2 == 0
        partner = jnp.where(even, pltpu.roll(p, tn - 1, 1), pltpu.roll(p, 1, 1))
        o_ref[...] = ((p * cos + partner * sin) * scale).astype(o_ref.dtype)

    @pl.when(jnp.logical_not(rope))
    def _():
        o_ref[...] = (p * scale).astype(o_ref.dtype)


def _diff_proj(geom, x, mod, gpre, w, cos_t, sin_t):
    d, tm = geom.d, geom.tm
    n = w.shape[1]
    tn = _pick_tile(d, PROJ_TILE, LANES)
    per_seq = geom.seq // tm
    kern = functools.partial(_diff_proj_kernel, lat_tiles=geom.lat_tiles, n_q=d // tn, n_qk=2 * d // tn)
    return pl.pallas_call(
        kern,
        grid=(geom.all_tiles, n // tn),
        in_specs=[
            pl.BlockSpec((tm, d), lambda i, j: (i, 0)),
            pl.BlockSpec((None, 3, d), lambda i, j: (geom.mod_row(i), 0, 0)),
            pl.BlockSpec((1, d), lambda i, j: (0, 0)),
            pl.BlockSpec((d, tn), lambda i, j: (0, j)),
            pl.BlockSpec((tm, LANES), lambda i, j: (i % per_seq, 0)),
            pl.BlockSpec((tm, LANES), lambda i, j: (i % per_seq, 0)),
        ],
        out_specs=pl.BlockSpec((tm, tn), lambda i, j: (i, j)),
        out_shape=jax.ShapeDtypeStruct((geom.rows, n), BF16),
        scratch_shapes=[pltpu.VMEM((tm, d), BF16)],
        compiler_params=_params(("parallel", "arbitrary")),
        name="diff_proj",
    )(x, mod, gpre, w, cos_t, sin_t)


def _diff_attn_kernel(*refs, n_seg, lam_init):
    q_ref = refs[0]
    k_refs = refs[1:1 + n_seg]
    v_refs = refs[1 + n_seg:1 + 2 * n_seg]
    lam_ref, subln_ref, o_ref = refs[1 + 2 * n_seg:]

    lp = lam_ref[...]
    lam = (jnp.exp(jnp.sum(lp[0:1, :] * lp[1:2, :], axis=1, keepdims=True))
           - jnp.exp(jnp.sum(lp[2:3, :] * lp[3:4, :], axis=1, keepdims=True)) + lam_init)

    q = q_ref[...]
    lane = lax.broadcasted_iota(jnp.int32, q.shape, 1)
    zero = jnp.zeros_like(q)
    comps = (jnp.where(lane < DIFF_HD, q, zero), jnp.where(lane >= DIFF_HD, q, zero))

    weights = []
    for m in range(2):
        s = [_dot_nt(comps[m], k_ref[...]) for k_ref in k_refs]
        mx = functools.reduce(jnp.maximum, [jnp.max(t, axis=1, keepdims=True) for t in s])
        e = [jnp.exp(t - mx) for t in s]
        z = functools.reduce(jnp.add, [jnp.sum(t, axis=1, keepdims=True) for t in e])
        weights.append((e, 1.0 / z))
    (e0, r0), (e1, r1) = weights
    r1 = lam * r1
    o = None
    for t in range(n_seg):
        a = (e0[t] * r0 - e1[t] * r1).astype(BF16)
        term = _dot(a, v_refs[t][...])
        o = term if o is None else o + term
    o_ref[...] = (_rms(o, subln_ref[...]) * (1.0 - lam_init)).astype(o_ref.dtype)


def _diff_attn(geom, qkv, lam_p, subln, lam_init, heads, context_queries):
    b, seq, ctx, d = geom.batch, geom.seq, geom.ctx, geom.d
    vd = 2 * DIFF_HD
    cb = geom.n_lat // ctx

    def lat(part):
        return pl.BlockSpec((seq, vd), lambda bi, hi, qi: (bi, part * heads + hi))

    def cx(part):
        return pl.BlockSpec((ctx, vd), lambda bi, hi, qi: (cb + bi, part * heads + hi))

    if context_queries:
        tq = _pick_tile(ctx, Q_TILE, SUBLANES)
        q0 = geom.n_lat // tq
        nq = ctx // tq
        k_specs, v_specs = [cx(1)], [cx(2)]
        out_rows = geom.n_ctx
    else:
        tq = _pick_tile(seq, Q_TILE, SUBLANES)
        q0 = 0
        nq = seq // tq
        k_specs, v_specs = [cx(1), lat(1)], [cx(2), lat(2)]
        out_rows = geom.n_lat
    n_seg = len(k_specs)
    in_specs = ([pl.BlockSpec((tq, vd), lambda bi, hi, qi: (q0 + bi * nq + qi, hi))] + k_specs + v_specs
                + [pl.BlockSpec(lam_p.shape, lambda bi, hi, qi: (0, 0)),
                   pl.BlockSpec((1, vd), lambda bi, hi, qi: (0, 0))])
    return pl.pallas_call(
        functools.partial(_diff_attn_kernel, n_seg=n_seg, lam_init=lam_init),
        grid=(b, heads, nq),
        in_specs=in_specs,
        out_specs=pl.BlockSpec((tq, vd), lambda bi, hi, qi: (bi * nq + qi, hi)),
        out_shape=jax.ShapeDtypeStruct((out_rows, d), BF16),
        compiler_params=_params(("parallel", "parallel", "arbitrary")),
        name="diff_attn_ctx" if context_queries else "diff_attn",
    )(*([qkv] * (1 + 2 * n_seg)), lam_p, subln)


def _rope_tables(seq):
    half = DIFF_HD // 2
    inv = ROPE_BASE ** (-jnp.arange(0, half, 2, dtype=F32) / half)
    pos = jnp.arange(seq)
    row = (pos // GRID_W).astype(F32)
    col = (pos % GRID_W).astype(F32)
    ang = jnp.concatenate([row[:, None] * inv, col[:, None] * inv], axis=-1)
    cos = jnp.repeat(jnp.cos(ang), 2, axis=-1)
    sin = jnp.repeat(jnp.sin(ang), 2, axis=-1) * jnp.tile(jnp.array([-1.0, 1.0], F32), half)
    return jnp.tile(cos, (1, 2)), jnp.tile(sin, (1, 2))


def kernel(x, c, ctx, c_ctx, w_mod, b_mod, norm_pre, norm_post, ffn_w_in, ffn_w_out, gdn_w_in, gdn_conv, gdn_A_log,
           gdn_dt_bias, gdn_o_gain, gdn_w_out, diff_w_in, diff_lambda, diff_subln, diff_w_out):
    batch, seq, d = x.shape
    n_ctx_tok = ctx.shape[1]
    depth = w_mod.shape[0]
    heads = d // GDN_DK
    geom = _Geom(batch, seq, n_ctx_tok, d)

    mod_rows = -(-(batch + 1) // SUBLANES) * SUBLANES
    cvec = jnp.concatenate([c, c_ctx[None], jnp.zeros((mod_rows - batch - 1, d), F32)], axis=0)
    mod = _modulation(cvec, w_mod, b_mod).reshape(depth, mod_rows, N_SUB, 3, d)

    cos_t, sin_t = _rope_tables(seq)
    xs = jnp.concatenate([x.reshape(batch * seq, d), ctx.reshape(batch * n_ctx_tok, d)], axis=0)

    for l in range(depth):
        last = l == depth - 1
        i = l // 2
        gpre = norm_pre[l].reshape(N_SUB, 1, d)
        gpost = norm_post[l].reshape(N_SUB, 1, d)
        w_in16 = ffn_w_in[l].astype(BF16)
        w_out16 = ffn_w_out[l].astype(BF16)

        xs = _ffn(geom, xs, mod[l, :, 0], gpre[0], gpost[0], w_in16[0], w_out16[0], geom.all_tiles)

        if l % 2 == 0:
            w = gdn_w_in[i]
            w_ab = jnp.pad(w[:, 4 * d:], ((0, 0), (0, LANES - 4 * heads))).astype(BF16)
            pad = (0, LANES - 2 * heads)
            alog_row = jnp.pad(gdn_A_log[i].reshape(-1), pad).reshape(1, LANES)
            dtb_row = jnp.pad(gdn_dt_bias[i].reshape(-1), pad).reshape(1, LANES)
            p, gb = _gdn_proj(geom, xs, mod[l, :, 1], gpre[1], w[:, :4 * d].astype(BF16), gdn_conv[i], w_ab,
                              alog_row, dtb_row, heads)
            yl, yc = _gdn_scan(geom, p, gb, gdn_o_gain[i].reshape(1, GDN_DK), heads)
            w_o = gdn_w_out[i].astype(BF16)
        else:
            lam_init = 0.8 - 0.6 * math.exp(-0.3 * l)
            qkv = _diff_proj(geom, xs, mod[l, :, 1], gpre[1], diff_w_in[i].astype(BF16), cos_t, sin_t)
            subln = diff_subln[i].reshape(1, 2 * DIFF_HD)
            yl = _diff_attn(geom, qkv, diff_lambda[i], subln, lam_init, heads, context_queries=False)
            yc = None if last else _diff_attn(geom, qkv, diff_lambda[i], subln, lam_init, heads, context_queries=True)
            w_o = diff_w_out[i].astype(BF16)

        xs = _outproj(geom, xs, yl, None if last else yc, mod[l, :, 1], gpost[1], w_o)
        xs = _ffn(geom, xs, mod[l, :, 2], gpre[2], gpost[2], w_in16[1], w_out16[1],
                  geom.lat_tiles if last else geom.all_tiles)

    return xs.reshape(batch, seq, d)
```

```python
import functools
import math

import jax
import jax.numpy as jnp
from jax import lax
from jax.experimental import pallas as pl
from jax.experimental.pallas import tpu as pltpu

F32 = jnp.float32
BF16 = jnp.bfloat16

NORM_EPS = 1e-6
MACARON_W = 0.5
N_SUB = 3
GRID_W = 64
GDN_DK = 128
GDN_CONV = 5
DIFF_HD = 64
ROPE_BASE = 10000.0

LANES = 128
SUBLANES = 8
VMEM_LIMIT_BYTES = 56 * 1024 * 1024

ROW_TILE = 1024
FF_TILE = 256
PROJ_TILE = 512
GDN_CHUNK = 256
GDN_PREP_GROUP = 3
Q_TILE = 512
HALO = SUBLANES


def _pick_tile(n, target, align):
    best = None
    t = align
    while t <= min(n, target):
        if n % t == 0:
            best = t
        t += align
    assert best is not None, (n, target, align)
    return best


def _params(semantics):
    return pltpu.CompilerParams(dimension_semantics=semantics, vmem_limit_bytes=VMEM_LIMIT_BYTES)


def _dot(a, b):
    return jnp.dot(a, b, preferred_element_type=F32)


def _dot_nt(a, b):
    return lax.dot_general(a, b, (((1,), (1,)), ((), ())), preferred_element_type=F32)


def _dot_tn(a, b):
    return lax.dot_general(a, b, (((0,), (0,)), ((), ())), preferred_element_type=F32)


def _split3(x):
    x1 = x.astype(BF16)
    r1 = x - x1.astype(F32)
    x2 = r1.astype(BF16)
    x3 = (r1 - x2.astype(F32)).astype(BF16)
    return x1, x2, x3


def _rms(x, gain):
    return x * lax.rsqrt(jnp.mean(x * x, axis=-1, keepdims=True) + NORM_EPS) * gain


def _silu(x):
    return x * jax.nn.sigmoid(x)


def _pre(x, mod_ref, gain_ref):
    return _rms(x, gain_ref[...]) * (1.0 + mod_ref[1:2, :]) + mod_ref[0:1, :]


def _mod_kernel(c_ref, w_ref, b_ref, o_ref):
    c = c_ref[...]
    s = _silu(c).astype(BF16)
    o_ref[...] = _dot(s, w_ref[...].astype(BF16)) + b_ref[...]


def _modulation(cvec, w_mod, b_mod):
    depth, d, n = w_mod.shape
    rows = cvec.shape[0]
    tn = _pick_tile(n, 1152, LANES)
    return pl.pallas_call(
        _mod_kernel,
        grid=(depth, n // tn),
        in_specs=[
            pl.BlockSpec((rows, d), lambda l, j: (0, 0)),
            pl.BlockSpec((None, d, tn), lambda l, j: (l, 0, j)),
            pl.BlockSpec((None, 1, tn), lambda l, j: (l, 0, j)),
        ],
        out_specs=pl.BlockSpec((None, rows, tn), lambda l, j: (l, 0, j)),
        out_shape=jax.ShapeDtypeStruct((depth, rows, n), F32),
        compiler_params=_params(("parallel", "arbitrary")),
        name="modulation",
    )(cvec, w_mod, b_mod.reshape(depth, 1, n))


class _Geom:
    def __init__(self, batch, seq, ctx, d):
        self.batch, self.seq, self.ctx, self.d = batch, seq, ctx, d
        self.n_lat = batch * seq
        self.n_ctx = batch * ctx
        self.rows = self.n_lat + self.n_ctx
        self.tm = _pick_tile(math.gcd(seq, self.n_ctx), ROW_TILE, SUBLANES)
        self.lat_tiles = self.n_lat // self.tm
        self.all_tiles = self.rows // self.tm
        assert self.n_lat % ctx == 0

    def mod_row(self, i):
        return jnp.where(i < self.lat_tiles, (i * self.tm) // self.seq, self.batch)


def _ffn_kernel(x_ref, mod_ref, gpre_ref, gpost_ref, win_ref, wout_ref, o_ref, acc_ref, *, tf):
    dff = wout_ref.shape[0]
    x = x_ref[...]
    h = _pre(x, mod_ref, gpre_ref).astype(BF16)
    for c in range(dff // tf):
        g = _dot(h, win_ref[:, c * tf:(c + 1) * tf])
        u = _dot(h, win_ref[:, dff + c * tf:dff + (c + 1) * tf])
        a = (_silu(g) * u).astype(BF16)
        y = _dot(a, wout_ref[c * tf:(c + 1) * tf, :])
        if c == 0:
            acc_ref[...] = y
        else:
            acc_ref[...] += y
    r = _rms(acc_ref[...], gpost_ref[...])
    o_ref[...] = x + MACARON_W * mod_ref[2:3, :] * r


def _ffn(geom, x, mod, gpre, gpost, w_in, w_out, n_tiles):
    d, tm = geom.d, geom.tm
    dff = w_out.shape[0]
    tf = _pick_tile(dff, FF_TILE, LANES)
    return pl.pallas_call(
        functools.partial(_ffn_kernel, tf=tf),
        grid=(n_tiles,),
        in_specs=[
            pl.BlockSpec((tm, d), lambda i: (i, 0)),
            pl.BlockSpec((None, 3, d), lambda i: (geom.mod_row(i), 0, 0)),
            pl.BlockSpec((1, d), lambda i: (0, 0)),
            pl.BlockSpec((1, d), lambda i: (0, 0)),
            pl.BlockSpec(memory_space=pltpu.VMEM),
            pl.BlockSpec(memory_space=pltpu.VMEM),
        ],
        out_specs=pl.BlockSpec((tm, d), lambda i: (i, 0)),
        out_shape=jax.ShapeDtypeStruct((n_tiles * tm, d), F32),
        scratch_shapes=[pltpu.VMEM((tm, d), F32)],
        compiler_params=_params(("parallel",)),
        name="ffn",
    )(x, mod, gpre, gpost, w_in, w_out)


def _outproj_kernel(*refs, lat_tiles, has_ctx):
    if has_ctx:
        x_ref, yl_ref, yc_ref, mod_ref, gpost_ref, w_ref, o_ref = refs
        y = jnp.where(pl.program_id(0) < lat_tiles, yl_ref[...], yc_ref[...])
    else:
        x_ref, yl_ref, mod_ref, gpost_ref, w_ref, o_ref = refs
        y = yl_ref[...]
    r = _rms(_dot(y, w_ref[...]), gpost_ref[...])
    o_ref[...] = x_ref[...] + mod_ref[2:3, :] * r


def _outproj(geom, x, yl, yc, mod, gpost, w):
    d, tm = geom.d, geom.tm
    has_ctx = yc is not None
    n_tiles = geom.all_tiles if has_ctx else geom.lat_tiles
    lt = geom.lat_tiles
    in_specs = [pl.BlockSpec((tm, d), lambda i: (i, 0)),
                pl.BlockSpec((tm, d), lambda i: (jnp.minimum(i, lt - 1), 0))]
    args = [x, yl]
    if has_ctx:
        in_specs.append(pl.BlockSpec((tm, d), lambda i: (jnp.maximum(i - lt, 0), 0)))
        args.append(yc)
    in_specs += [
        pl.BlockSpec((None, 3, d), lambda i: (geom.mod_row(i), 0, 0)),
        pl.BlockSpec((1, d), lambda i: (0, 0)),
        pl.BlockSpec((d, d), lambda i: (0, 0)),
    ]
    args += [mod, gpost, w]
    return pl.pallas_call(
        functools.partial(_outproj_kernel, lat_tiles=lt, has_ctx=has_ctx),
        grid=(n_tiles,),
        in_specs=in_specs,
        out_specs=pl.BlockSpec((tm, d), lambda i: (i, 0)),
        out_shape=jax.ShapeDtypeStruct((n_tiles * tm, d), F32),
        compiler_params=_params(("parallel",)),
        name="outproj",
    )(*args)


def _gdn_proj_kernel(x_ref, xp_ref, xn_ref, mod_ref, gpre_ref, w_ref, cw_ref, wab_ref, alog_ref, dtb_ref,
                     p_ref, gb_ref, h_ref, pbuf_ref, *, tm, seq, ctx, lat_tiles, n_qk, n_conv, heads):
    i = pl.program_id(0)
    j = pl.program_id(1)
    latent = i < lat_tiles

    @pl.when(j == 0)
    def _():
        first = jnp.logical_and(latent, (i * tm) % seq == 0)
        final = jnp.logical_and(latent, ((i + 1) * tm) % seq == 0)
        hp = _pre(xp_ref[...], mod_ref, gpre_ref)
        hn = _pre(xn_ref[...], mod_ref, gpre_ref)
        h_ref[0:HALO, :] = jnp.where(first, 0.0, hp).astype(BF16)
        hm = _pre(x_ref[...], mod_ref, gpre_ref).astype(BF16)
        h_ref[HALO:HALO + tm, :] = hm
        h_ref[HALO + tm:, :] = jnp.where(final, 0.0, hn).astype(BF16)
        ab = _dot(hm, wab_ref[...])
        lane = lax.broadcasted_iota(jnp.int32, ab.shape, 1)
        z = ab + dtb_ref[...]
        softplus = jnp.maximum(z, 0.0) + jnp.log1p(jnp.exp(-jnp.abs(z)))
        logdecay = -jnp.exp(alog_ref[...]) * softplus
        gb_ref[...] = jnp.where(lane < 2 * heads, logdecay, jax.nn.sigmoid(ab))

    p = _dot(h_ref[...], w_ref[...])

    @pl.when(j >= n_conv)
    def _():
        p_ref[...] = p[HALO:HALO + tm, :]

    def finish(acc):
        z = _silu(acc)

        @pl.when(j >= n_qk)
        def _():
            p_ref[...] = z

        @pl.when(j < n_qk)
        def _():
            scale = jnp.where(j < n_qk // 2, GDN_DK ** -0.5, 1.0)
            for c in range(z.shape[1] // GDN_DK):
                zc = z[:, c * GDN_DK:(c + 1) * GDN_DK]
                inv = lax.rsqrt(jnp.sum(zc * zc, axis=-1, keepdims=True) + NORM_EPS) * scale
                p_ref[:, c * GDN_DK:(c + 1) * GDN_DK] = zc * inv

    def taps():
        for t in range(GDN_CONV):
            off = t - GDN_CONV // 2
            yield off, pbuf_ref[HALO + off:HALO + off + tm, :], cw_ref[t:t + 1, :]

    @pl.when(jnp.logical_and(j < n_conv, latent))
    def _():
        pbuf_ref[...] = p
        acc = None
        for _, tap, cw in taps():
            acc = tap * cw if acc is None else acc + tap * cw
        finish(acc)

    @pl.when(jnp.logical_and(j < n_conv, jnp.logical_not(latent)))
    def _():
        pbuf_ref[...] = p
        r = (i * tm) % ctx + lax.broadcasted_iota(jnp.int32, (tm, 1), 0)
        wraps = jnp.floor((r.astype(F32) + 0.5) / float(ctx)).astype(jnp.int32)
        pos = r - wraps * ctx
        acc = None
        for off, tap, cw in taps():
            ok = jnp.logical_and(pos + off >= 0, pos + off < ctx)
            term = jnp.where(ok, tap, 0.0) * cw
            acc = term if acc is None else acc + term
        finish(acc)


def _gdn_proj(geom, x, mod, gpre, w_main, conv_w, w_ab, alog_row, dtb_row, heads):
    d, tm = geom.d, geom.tm
    n = w_main.shape[1]
    tn = _pick_tile(d, PROJ_TILE, GDN_DK)
    nj = n // tn
    n_conv = 3 * d // tn
    n_qk = 2 * d // tn
    hb = tm // HALO
    last_hb = geom.rows // HALO - 1
    kern = functools.partial(_gdn_proj_kernel, tm=tm, seq=geom.seq, ctx=geom.ctx, lat_tiles=geom.lat_tiles,
                             n_qk=n_qk, n_conv=n_conv, heads=heads)
    return pl.pallas_call(
        kern,
        grid=(geom.all_tiles, nj),
        in_specs=[
            pl.BlockSpec((tm, d), lambda i, j: (i, 0)),
            pl.BlockSpec((HALO, d), lambda i, j: (jnp.maximum(i * hb - 1, 0), 0)),
            pl.BlockSpec((HALO, d), lambda i, j: (jnp.minimum((i + 1) * hb, last_hb), 0)),
            pl.BlockSpec((None, 3, d), lambda i, j: (geom.mod_row(i), 0, 0)),
            pl.BlockSpec((1, d), lambda i, j: (0, 0)),
            pl.BlockSpec((d, tn), lambda i, j: (0, j)),
            pl.BlockSpec((GDN_CONV, tn), lambda i, j: (0, jnp.minimum(j, n_conv - 1))),
            pl.BlockSpec((d, LANES), lambda i, j: (0, 0)),
            pl.BlockSpec((1, LANES), lambda i, j: (0, 0)),
            pl.BlockSpec((1, LANES), lambda i, j: (0, 0)),
        ],
        out_specs=[
            pl.BlockSpec((tm, tn), lambda i, j: (i, j)),
            pl.BlockSpec((tm, LANES), lambda i, j: (i, 0)),
        ],
        out_shape=[
            jax.ShapeDtypeStruct((geom.rows, n), F32),
            jax.ShapeDtypeStruct((geom.rows, LANES), F32),
        ],
        scratch_shapes=[pltpu.VMEM((tm + 2 * HALO, d), BF16), pltpu.VMEM((tm + 2 * HALO, tn), F32)],
        compiler_params=_params(("parallel", "arbitrary")),
        name="gdn_proj",
    )(x, x, x, mod, gpre, w_main, conv_w, w_ab, alog_row, dtb_row)


def _gdn_prep(chains, h, heads):
    n = len(chains)
    qs, ks, vs, gbs, revs = zip(*chains)
    c = qs[0].shape[0]
    row = lax.broadcasted_iota(jnp.int32, (c, c), 0)
    col = lax.broadcasted_iota(jnp.int32, (c, c), 1)
    lane = lax.broadcasted_iota(jnp.int32, gbs[0].shape, 1)
    eye = jnp.where(row == col, 1.0, 0.0)
    incl = {False: row >= col, True: row <= col}
    strict = {False: row > col, True: row < col}
    tri = {r: jnp.where(m, 1.0, 0.0).astype(BF16) for r, m in incl.items()}
    first_lane = jnp.where(lax.broadcasted_iota(jnp.int32, (SUBLANES, LANES), 1) == 0, 1.0, 0.0).astype(BF16)
    rng = range(n)

    def pick(gb, idx):
        colv = jnp.sum(jnp.where(lane == idx, gb, 0.0), axis=1, keepdims=True)
        return jnp.broadcast_to(colv, gb.shape)

    g = [pick(gbs[i], int(revs[i]) * heads + h) for i in rng]
    beta = [pick(gbs[i], 2 * heads + int(revs[i]) * heads + h) for i in rng]

    gp = [_split3(x) for x in g]
    gc = [sum(_dot(tri[revs[i]], p) for p in gp[i]) for i in rng]
    cp = [_split3(x) for x in gc]
    grow = [sum(_dot_nt(first_lane, p) for p in cp[i]) for i in rng]
    gam = []
    for i in rng:
        gr = jnp.broadcast_to(grow[i][0:1, :], (c, c))
        gcol = jnp.concatenate([gc[i]] * (c // LANES), axis=1)
        m = incl[revs[i]]
        gam.append(jnp.where(m, jnp.exp(jnp.where(m, gcol - gr, 0.0)), 0.0))

    kb = [ks[i] * beta[i] for i in rng]
    k16 = [x.astype(BF16) for x in ks]
    kk = [_dot_nt(kb[i].astype(BF16), k16[i]) for i in rng]
    qk = [_dot_nt(qs[i].astype(BF16), k16[i]) for i in rng]
    a = [jnp.where(strict[revs[i]], kk[i] * gam[i], 0.0) for i in rng]
    intra = [(qk[i] * gam[i]).astype(BF16) for i in rng]
    eg = [jnp.exp(x) for x in gc]

    leaf = SUBLANES.bit_length() - 1
    in_leaf = (row >> leaf) == (col >> leaf)
    a8 = [jnp.where(in_leaf, x, 0.0) for x in a]
    a8b = [x.astype(BF16) for x in a8]
    sq = [_dot(x, x).astype(BF16) for x in a8b]
    t = [eye - x for x in a8]
    t = [t[i] + _dot(t[i].astype(BF16), sq[i]) for i in rng]
    qd = [_dot(x, x).astype(BF16) for x in sq]
    t = [t[i] + _dot(t[i].astype(BF16), qd[i]) for i in rng]
    for lvl in range(leaf, c.bit_length() - 1):
        rb = row >> lvl
        cb = col >> lvl
        off = {False: jnp.logical_and(rb == cb + 1, (rb & 1) == 1),
               True: jnp.logical_and(cb == rb + 1, (rb & 1) == 0)}
        t16 = [x.astype(BF16) for x in t]
        inner = [_dot(jnp.where(off[revs[i]], a[i], 0.0).astype(BF16), t16[i]).astype(BF16) for i in rng]
        t = [t[i] - _dot(t16[i], inner[i]) for i in rng]

    r = [jnp.concatenate([vs[i] * beta[i], kb[i] * eg[i]], axis=1) for i in rng]
    r = [r[i] + _dot((t[i] - eye).astype(BF16), r[i].astype(BF16)) for i in rng]

    out = []
    for i in rng:
        g_last = gc[i][0:1, :] if revs[i] else gc[i][c - 1:c, :]
        k_tail = ks[i] * jnp.exp(g_last - gc[i])
        out.append((r[i][:, :GDN_DK], r[i][:, GDN_DK:].astype(BF16), (qs[i] * eg[i]).astype(BF16),
                    k_tail.astype(BF16), intra[i], jnp.exp(g_last)))
    return out


def _gdn_step(u, w, q_dec, k_tail, intra, decay, s):
    s16 = s.astype(BF16)
    vn16 = (u - _dot(w, s16)).astype(BF16)
    o = _dot(q_dec, s16) + _dot(intra, vn16)
    return o, s * decay + _dot_tn(k_tail, vn16)


def _gdn_scan_kernel(ql, kl, vl, gl, gbl, qc, kc, vc, gc, gbc, og_ref, yl_ref, yc_ref,
                     q_s, k_s, v_s, gb_s, u_s, w_s, qd_s, kt_s, in_s, dec_s, o_s, *, heads, chunk, group):
    h = pl.program_id(1)
    n_ctx = qc.shape[0]
    n_tok = q_s.shape[0]
    cc = n_ctx // chunk
    nc = n_tok // chunk

    for dst, c_ref, l_ref in ((q_s, qc, ql), (k_s, kc, kl), (v_s, vc, vl), (gb_s, gbc, gbl)):
        dst[0:n_ctx, :] = c_ref[...]
        dst[n_ctx:, :] = l_ref[...]

    def prep_body(it, carry):
        chains, where = [], []
        for j in range(group):
            ci = it * group + j
            rows = pl.ds(pl.multiple_of(ci * chunk, chunk), chunk)
            for d in range(2):
                chains.append((q_s[rows, :], k_s[rows, :], v_s[rows, :], gb_s[rows, :], bool(d)))
                where.append((d, rows, pl.ds(pl.multiple_of(ci * SUBLANES, SUBLANES), SUBLANES)))
        for (d, rows, dec_rows), (u, w, qd, kt, intra, dec) in zip(where, _gdn_prep(chains, h, heads)):
            u_s[d, rows, :] = u
            w_s[d, rows, :] = w
            qd_s[d, rows, :] = qd
            kt_s[d, rows, :] = kt
            in_s[d, rows, :] = intra
            dec_s[d, dec_rows, :] = jnp.broadcast_to(dec, (SUBLANES, GDN_DK))
        return carry

    lax.fori_loop(0, nc // group, prep_body, 0)

    def scan_body(n, carry):
        out = []
        for d in range(2):
            ci = n if d == 0 else jnp.where(n < cc, cc - 1 - n, nc - 1 - (n - cc))
            rows = pl.ds(pl.multiple_of(ci * chunk, chunk), chunk)
            dec = dec_s[d, pl.ds(pl.multiple_of(ci * SUBLANES, SUBLANES), SUBLANES), :][0:1, :]
            o, s_next = _gdn_step(u_s[d, rows, :], w_s[d, rows, :], qd_s[d, rows, :], kt_s[d, rows, :],
                                  in_s[d, rows, :], dec, carry[d])
            o_s[d, rows, :] = o
            out.append(s_next)
        return tuple(out)

    zero = jnp.zeros((GDN_DK, GDN_DK), F32)
    lax.fori_loop(0, nc, scan_body, (zero, zero))

    og = og_ref[...]
    yc_ref[...] = (_rms(o_s[0, 0:n_ctx, :] + o_s[1, 0:n_ctx, :], og) * _silu(gc[...])).astype(yc_ref.dtype)
    yl_ref[...] = (_rms(o_s[0, n_ctx:, :] + o_s[1, n_ctx:, :], og) * _silu(gl[...])).astype(yl_ref.dtype)


def _gdn_scan(geom, p, gb, o_gain, heads):
    b, seq, ctx, d = geom.batch, geom.seq, geom.ctx, geom.d
    chunk = _pick_tile(math.gcd(seq, ctx), GDN_CHUNK, LANES)
    n_tok = ctx + seq
    nc = n_tok // chunk
    group = max(g for g in range(1, GDN_PREP_GROUP + 1) if nc % g == 0)
    cb = geom.n_lat // ctx

    def lat(part):
        return pl.BlockSpec((seq, GDN_DK), lambda bi, hi: (bi, part * heads + hi))

    def cx(part):
        return pl.BlockSpec((ctx, GDN_DK), lambda bi, hi: (cb + bi, part * heads + hi))

    in_specs = [lat(0), lat(1), lat(2), lat(3), pl.BlockSpec((seq, LANES), lambda bi, hi: (bi, 0)),
                cx(0), cx(1), cx(2), cx(3), pl.BlockSpec((ctx, LANES), lambda bi, hi: (cb + bi, 0)),
                pl.BlockSpec((1, GDN_DK), lambda bi, hi: (0, 0))]
    tok = lambda dt: pltpu.VMEM((n_tok, GDN_DK), dt)
    per_dir = lambda cols, dt: pltpu.VMEM((2, n_tok, cols), dt)
    return pl.pallas_call(
        functools.partial(_gdn_scan_kernel, heads=heads, chunk=chunk, group=group),
        grid=(b, heads),
        in_specs=in_specs,
        out_specs=[pl.BlockSpec((seq, GDN_DK), lambda bi, hi: (bi, hi)),
                   pl.BlockSpec((ctx, GDN_DK), lambda bi, hi: (bi, hi))],
        out_shape=[jax.ShapeDtypeStruct((geom.n_lat, d), BF16), jax.ShapeDtypeStruct((geom.n_ctx, d), BF16)],
        scratch_shapes=[tok(F32), tok(F32), tok(F32), tok(F32),
                        per_dir(GDN_DK, F32), per_dir(GDN_DK, BF16), per_dir(GDN_DK, BF16),
                        per_dir(GDN_DK, BF16), per_dir(chunk, BF16),
                        pltpu.VMEM((2, nc * SUBLANES, GDN_DK), F32), per_dir(GDN_DK, F32)],
        compiler_params=_params(("parallel", "parallel")),
        name="gdn_scan",
    )(p, p, p, p, gb, p, p, p, p, gb, o_gain)


PAIR = 2 * LANES
HALF_HD = DIFF_HD // 2


def _pair_layout_columns(d):
    idx = []
    for p in range(d // PAIR):
        for half in range(2):
            for hh in range(2):
                for m in range(2):
                    for i in range(HALF_HD):
                        idx.append((2 * p + hh) * LANES + m * DIFF_HD + 2 * i + half)
    return jnp.asarray(idx, jnp.int32)


def _diff_proj_kernel(x_ref, mod_ref, gpre_ref, w_ref, cos_ref, sin_ref, o_ref, h_ref, *, lat_tiles, n_q, n_qk):
    i = pl.program_id(0)
    j = pl.program_id(1)

    @pl.when(j == 0)
    def _():
        h_ref[...] = _pre(x_ref[...], mod_ref, gpre_ref).astype(BF16)

    p = _dot(h_ref[...], w_ref[...])
    scale = jnp.where(j < n_q, DIFF_HD ** -0.5 * math.log2(math.e), 1.0)
    rope = jnp.logical_and(i < lat_tiles, j < n_qk)

    @pl.when(rope)
    def _():
        cos = cos_ref[...]
        sin = sin_ref[...]
        for b in range(p.shape[1] // PAIR):
            x1 = p[:, b * PAIR:b * PAIR + LANES]
            x2 = p[:, b * PAIR + LANES:(b + 1) * PAIR]
            o_ref[:, b * PAIR:b * PAIR + LANES] = ((x1 * cos - x2 * sin) * scale).astype(o_ref.dtype)
            o_ref[:, b * PAIR + LANES:(b + 1) * PAIR] = ((x1 * sin + x2 * cos) * scale).astype(o_ref.dtype)

    @pl.when(jnp.logical_not(rope))
    def _():
        o_ref[...] = (p * scale).astype(o_ref.dtype)


def _diff_proj(geom, x, mod, gpre, w, cos_t, sin_t):
    d, tm = geom.d, geom.tm
    n = w.shape[1]
    tn = _pick_tile(d, PROJ_TILE, PAIR)
    per_seq = geom.seq // tm
    kern = functools.partial(_diff_proj_kernel, lat_tiles=geom.lat_tiles, n_q=d // tn, n_qk=2 * d // tn)
    return pl.pallas_call(
        kern,
        grid=(geom.all_tiles, n // tn),
        in_specs=[
            pl.BlockSpec((tm, d), lambda i, j: (i, 0)),
            pl.BlockSpec((None, 3, d), lambda i, j: (geom.mod_row(i), 0, 0)),
            pl.BlockSpec((1, d), lambda i, j: (0, 0)),
            pl.BlockSpec((d, tn), lambda i, j: (0, j)),
            pl.BlockSpec((tm, LANES), lambda i, j: (i % per_seq, 0)),
            pl.BlockSpec((tm, LANES), lambda i, j: (i % per_seq, 0)),
        ],
        out_specs=pl.BlockSpec((tm, tn), lambda i, j: (i, j)),
        out_shape=jax.ShapeDtypeStruct((geom.rows, n), BF16),
        scratch_shapes=[pltpu.VMEM((tm, d), BF16)],
        compiler_params=_params(("parallel", "arbitrary")),
        name="diff_proj",
    )(x, mod, gpre, w, cos_t, sin_t)


def _diff_attn_kernel(*refs, n_seg, lam_init):
    q_ref = refs[0]
    k_refs = refs[1:1 + n_seg]
    v_refs = refs[1 + n_seg:1 + 2 * n_seg]
    lam_ref, subln_ref, o_ref = refs[1 + 2 * n_seg:]

    lp = lam_ref[...]
    lam = (jnp.exp(jnp.sum(lp[0:1, :] * lp[1:2, :], axis=1, keepdims=True))
           - jnp.exp(jnp.sum(lp[2:3, :] * lp[3:4, :], axis=1, keepdims=True)) + lam_init)

    q = q_ref[...]
    group = (lax.broadcasted_iota(jnp.int32, q.shape, 1) >> (HALF_HD.bit_length() - 1)) & 3
    zero = jnp.zeros_like(q)
    subln = subln_ref[...]

    def scores(n):
        qm = jnp.where(group == n, q, zero)
        return [_dot_nt(qm, k_ref[...]) for k_ref in k_refs]

    def weighted(n, s):
        hh = n // 2
        mx = functools.reduce(jnp.maximum, [jnp.max(t, axis=1, keepdims=True) for t in s])
        acc = None
        for t in range(n_seg):
            e = jnp.exp2(s[t] - mx).astype(BF16)
            vaug = jnp.concatenate([v_refs[t][:, hh * LANES:(hh + 1) * LANES],
                                    jnp.ones((v_refs[t].shape[0], LANES), BF16)], axis=1)
            term = _dot(e, vaug)
            acc = term if acc is None else acc + term
        return acc[:, :LANES] * (1.0 / acc[:, LANES:])

    s_next = scores(0)
    num = []
    for n in range(4):
        s_cur = s_next
        if n + 1 < 4:
            s_next = scores(n + 1)
        num.append(weighted(n, s_cur))
    for hh in range(2):
        o = num[2 * hh] - lam * num[2 * hh + 1]
        o_ref[:, hh * LANES:(hh + 1) * LANES] = (_rms(o, subln) * (1.0 - lam_init)).astype(o_ref.dtype)


def _diff_attn(geom, qkv, lam_p, subln, lam_init, context_queries):
    b, seq, ctx, d = geom.batch, geom.seq, geom.ctx, geom.d
    pairs = d // PAIR
    cb = geom.n_lat // ctx

    def lat(part):
        return pl.BlockSpec((seq, PAIR), lambda bi, pi, qi: (bi, part * pairs + pi))

    def cx(part):
        return pl.BlockSpec((ctx, PAIR), lambda bi, pi, qi: (cb + bi, part * pairs + pi))

    if context_queries:
        tq = _pick_tile(ctx, Q_TILE, SUBLANES)
        q0 = geom.n_lat // tq
        nq = ctx // tq
        k_specs, v_specs = [cx(1)], [cx(2)]
        out_rows = geom.n_ctx
    else:
        tq = _pick_tile(seq, Q_TILE, SUBLANES)
        q0 = 0
        nq = seq // tq
        k_specs, v_specs = [cx(1), lat(1)], [cx(2), lat(2)]
        out_rows = geom.n_lat
    n_seg = len(k_specs)
    in_specs = ([pl.BlockSpec((tq, PAIR), lambda bi, pi, qi: (q0 + bi * nq + qi, pi))] + k_specs + v_specs
                + [pl.BlockSpec(lam_p.shape, lambda bi, pi, qi: (0, 0)),
                   pl.BlockSpec((1, LANES), lambda bi, pi, qi: (0, 0))])
    return pl.pallas_call(
        functools.partial(_diff_attn_kernel, n_seg=n_seg, lam_init=lam_init),
        grid=(b, pairs, nq),
        in_specs=in_specs,
        out_specs=pl.BlockSpec((tq, PAIR), lambda bi, pi, qi: (bi * nq + qi, pi)),
        out_shape=jax.ShapeDtypeStruct((out_rows, d), BF16),
        compiler_params=_params(("parallel", "parallel", "arbitrary")),
        name="diff_attn_ctx" if context_queries else "diff_attn",
    )(*([qkv] * (1 + 2 * n_seg)), lam_p, subln)


def _rope_tables(seq):
    inv = ROPE_BASE ** (-jnp.arange(0, HALF_HD, 2, dtype=F32) / HALF_HD)
    pos = jnp.arange(seq)
    row = (pos // GRID_W).astype(F32)
    col = (pos % GRID_W).astype(F32)
    ang = jnp.concatenate([row[:, None] * inv, col[:, None] * inv], axis=-1)
    return jnp.tile(jnp.cos(ang), (1, LANES // HALF_HD)), jnp.tile(jnp.sin(ang), (1, LANES // HALF_HD))


def kernel(x, c, ctx, c_ctx, w_mod, b_mod, norm_pre, norm_post, ffn_w_in, ffn_w_out, gdn_w_in, gdn_conv, gdn_A_log,
           gdn_dt_bias, gdn_o_gain, gdn_w_out, diff_w_in, diff_lambda, diff_subln, diff_w_out):
    batch, seq, d = x.shape
    n_ctx_tok = ctx.shape[1]
    depth = w_mod.shape[0]
    heads = d // GDN_DK
    geom = _Geom(batch, seq, n_ctx_tok, d)

    mod_rows = -(-(batch + 1) // SUBLANES) * SUBLANES
    cvec = jnp.concatenate([c, c_ctx[None], jnp.zeros((mod_rows - batch - 1, d), F32)], axis=0)
    mod = _modulation(cvec, w_mod, b_mod).reshape(depth, mod_rows, N_SUB, 3, d)

    cos_t, sin_t = _rope_tables(seq)
    pair_cols = _pair_layout_columns(d)
    xs = jnp.concatenate([x.reshape(batch * seq, d), ctx.reshape(batch * n_ctx_tok, d)], axis=0)

    for l in range(depth):
        last = l == depth - 1
        i = l // 2
        gpre = norm_pre[l].reshape(N_SUB, 1, d)
        gpost = norm_post[l].reshape(N_SUB, 1, d)
        w_in16 = ffn_w_in[l].astype(BF16)
        w_out16 = ffn_w_out[l].astype(BF16)

        xs = _ffn(geom, xs, mod[l, :, 0], gpre[0], gpost[0], w_in16[0], w_out16[0], geom.all_tiles)

        if l % 2 == 0:
            w = gdn_w_in[i]
            w_ab = jnp.pad(w[:, 4 * d:], ((0, 0), (0, LANES - 4 * heads))).astype(BF16)
            pad = (0, LANES - 2 * heads)
            alog_row = jnp.pad(gdn_A_log[i].reshape(-1), pad).reshape(1, LANES)
            dtb_row = jnp.pad(gdn_dt_bias[i].reshape(-1), pad).reshape(1, LANES)
            p, gb = _gdn_proj(geom, xs, mod[l, :, 1], gpre[1], w[:, :4 * d].astype(BF16), gdn_conv[i], w_ab,
                              alog_row, dtb_row, heads)
            yl, yc = _gdn_scan(geom, p, gb, gdn_o_gain[i].reshape(1, GDN_DK), heads)
            w_o = gdn_w_out[i].astype(BF16)
        else:
            lam_init = 0.8 - 0.6 * math.exp(-0.3 * l)
            w = diff_w_in[i]
            w = jnp.concatenate([jnp.take(w[:, :d], pair_cols, axis=1), jnp.take(w[:, d:2 * d], pair_cols, axis=1),
                                 w[:, 2 * d:]], axis=1).astype(BF16)
            qkv = _diff_proj(geom, xs, mod[l, :, 1], gpre[1], w, cos_t, sin_t)
            subln = diff_subln[i].reshape(1, 2 * DIFF_HD)
            yl = _diff_attn(geom, qkv, diff_lambda[i], subln, lam_init, context_queries=False)
            yc = None if last else _diff_attn(geom, qkv, diff_lambda[i], subln, lam_init, context_queries=True)
            w_o = diff_w_out[i].astype(BF16)

        xs = _outproj(geom, xs, yl, None if last else yc, mod[l, :, 1], gpost[1], w_o)
        xs = _ffn(geom, xs, mod[l, :, 2], gpre[2], gpost[2], w_in16[1], w_out16[1],
                  geom.lat_tiles if last else geom.all_tiles)

    return xs.reshape(batch, seq, d)
```

```python
import functools
import math

import jax
import jax.numpy as jnp
from jax import lax
from jax.experimental import pallas as pl
from jax.experimental.pallas import tpu as pltpu

F32 = jnp.float32
BF16 = jnp.bfloat16

NORM_EPS = 1e-6
MACARON_W = 0.5
N_SUB = 3
GRID_W = 64
GDN_DK = 128
GDN_CONV = 5
DIFF_HD = 64
ROPE_BASE = 10000.0

LANES = 128
SUBLANES = 8
VMEM_LIMIT_BYTES = 56 * 1024 * 1024

ROW_TILE = 1024
FF_TILE = 256
PROJ_TILE = 512
GDN_CHUNK = 256
GDN_PREP_GROUP = 3
Q_TILE = 512
HALO = SUBLANES


def _pick_tile(n, target, align):
    best = None
    t = align
    while t <= min(n, target):
        if n % t == 0:
            best = t
        t += align
    assert best is not None, (n, target, align)
    return best


def _params(semantics):
    return pltpu.CompilerParams(dimension_semantics=semantics, vmem_limit_bytes=VMEM_LIMIT_BYTES)


def _dot(a, b):
    return jnp.dot(a, b, preferred_element_type=F32)


def _dot_nt(a, b):
    return lax.dot_general(a, b, (((1,), (1,)), ((), ())), preferred_element_type=F32)


def _dot_tn(a, b):
    return lax.dot_general(a, b, (((0,), (0,)), ((), ())), preferred_element_type=F32)


def _split3(x):
    x1 = x.astype(BF16)
    r1 = x - x1.astype(F32)
    x2 = r1.astype(BF16)
    x3 = (r1 - x2.astype(F32)).astype(BF16)
    return x1, x2, x3


def _rms(x, gain):
    return x * lax.rsqrt(jnp.mean(x * x, axis=-1, keepdims=True) + NORM_EPS) * gain


def _silu(x):
    return x * jax.nn.sigmoid(x)


def _pre(x, mod_ref, gain_ref):
    return _rms(x, gain_ref[...]) * (1.0 + mod_ref[1:2, :]) + mod_ref[0:1, :]


def _mod_kernel(c_ref, w_ref, b_ref, o_ref):
    c = c_ref[...]
    s = _silu(c).astype(BF16)
    o_ref[...] = _dot(s, w_ref[...].astype(BF16)) + b_ref[...]


def _modulation(cvec, w_mod, b_mod):
    depth, d, n = w_mod.shape
    rows = cvec.shape[0]
    tn = _pick_tile(n, 1152, LANES)
    return pl.pallas_call(
        _mod_kernel,
        grid=(depth, n // tn),
        in_specs=[
            pl.BlockSpec((rows, d), lambda l, j: (0, 0)),
            pl.BlockSpec((None, d, tn), lambda l, j: (l, 0, j)),
            pl.BlockSpec((None, 1, tn), lambda l, j: (l, 0, j)),
        ],
        out_specs=pl.BlockSpec((None, rows, tn), lambda l, j: (l, 0, j)),
        out_shape=jax.ShapeDtypeStruct((depth, rows, n), F32),
        compiler_params=_params(("parallel", "arbitrary")),
        name="modulation",
    )(cvec, w_mod, b_mod.reshape(depth, 1, n))


class _Geom:
    def __init__(self, batch, seq, ctx, d):
        self.batch, self.seq, self.ctx, self.d = batch, seq, ctx, d
        self.n_lat = batch * seq
        self.n_ctx = batch * ctx
        self.rows = self.n_lat + self.n_ctx
        self.tm = _pick_tile(math.gcd(seq, self.n_ctx), ROW_TILE, SUBLANES)
        self.lat_tiles = self.n_lat // self.tm
        self.all_tiles = self.rows // self.tm
        assert self.n_lat % ctx == 0

    def mod_row(self, i):
        return jnp.where(i < self.lat_tiles, (i * self.tm) // self.seq, self.batch)


def _ffn_kernel(x_ref, mod_ref, gpre_ref, gpost_ref, win_ref, wout_ref, o_ref, acc_ref, *, tf):
    dff = wout_ref.shape[0]
    x = x_ref[...]
    h = _pre(x, mod_ref, gpre_ref).astype(BF16)
    for c in range(dff // tf):
        g = _dot(h, win_ref[:, c * tf:(c + 1) * tf])
        u = _dot(h, win_ref[:, dff + c * tf:dff + (c + 1) * tf])
        a = (_silu(g) * u).astype(BF16)
        y = _dot(a, wout_ref[c * tf:(c + 1) * tf, :])
        if c == 0:
            acc_ref[...] = y
        else:
            acc_ref[...] += y
    r = _rms(acc_ref[...], gpost_ref[...])
    o_ref[...] = x + MACARON_W * mod_ref[2:3, :] * r


def _ffn(geom, x, mod, gpre, gpost, w_in, w_out, n_tiles):
    d, tm = geom.d, geom.tm
    dff = w_out.shape[0]
    tf = _pick_tile(dff, FF_TILE, LANES)
    return pl.pallas_call(
        functools.partial(_ffn_kernel, tf=tf),
        grid=(n_tiles,),
        in_specs=[
            pl.BlockSpec((tm, d), lambda i: (i, 0)),
            pl.BlockSpec((None, 3, d), lambda i: (geom.mod_row(i), 0, 0)),
            pl.BlockSpec((1, d), lambda i: (0, 0)),
            pl.BlockSpec((1, d), lambda i: (0, 0)),
            pl.BlockSpec(memory_space=pltpu.VMEM),
            pl.BlockSpec(memory_space=pltpu.VMEM),
        ],
        out_specs=pl.BlockSpec((tm, d), lambda i: (i, 0)),
        out_shape=jax.ShapeDtypeStruct((n_tiles * tm, d), F32),
        scratch_shapes=[pltpu.VMEM((tm, d), F32)],
        compiler_params=_params(("parallel",)),
        name="ffn",
    )(x, mod, gpre, gpost, w_in, w_out)


def _outproj_kernel(*refs, lat_tiles, has_ctx):
    if has_ctx:
        x_ref, yl_ref, yc_ref, mod_ref, gpost_ref, w_ref, o_ref = refs
        y = jnp.where(pl.program_id(0) < lat_tiles, yl_ref[...], yc_ref[...])
    else:
        x_ref, yl_ref, mod_ref, gpost_ref, w_ref, o_ref = refs
        y = yl_ref[...]
    r = _rms(_dot(y, w_ref[...]), gpost_ref[...])
    o_ref[...] = x_ref[...] + mod_ref[2:3, :] * r


def _outproj(geom, x, yl, yc, mod, gpost, w):
    d, tm = geom.d, geom.tm
    has_ctx = yc is not None
    n_tiles = geom.all_tiles if has_ctx else geom.lat_tiles
    lt = geom.lat_tiles
    in_specs = [pl.BlockSpec((tm, d), lambda i: (i, 0)),
                pl.BlockSpec((tm, d), lambda i: (jnp.minimum(i, lt - 1), 0))]
    args = [x, yl]
    if has_ctx:
        in_specs.append(pl.BlockSpec((tm, d), lambda i: (jnp.maximum(i - lt, 0), 0)))
        args.append(yc)
    in_specs += [
        pl.BlockSpec((None, 3, d), lambda i: (geom.mod_row(i), 0, 0)),
        pl.BlockSpec((1, d), lambda i: (0, 0)),
        pl.BlockSpec((d, d), lambda i: (0, 0)),
    ]
    args += [mod, gpost, w]
    return pl.pallas_call(
        functools.partial(_outproj_kernel, lat_tiles=lt, has_ctx=has_ctx),
        grid=(n_tiles,),
        in_specs=in_specs,
        out_specs=pl.BlockSpec((tm, d), lambda i: (i, 0)),
        out_shape=jax.ShapeDtypeStruct((n_tiles * tm, d), F32),
        compiler_params=_params(("parallel",)),
        name="outproj",
    )(*args)


def _gdn_proj_kernel(x_ref, xp_ref, xn_ref, mod_ref, gpre_ref, w_ref, cw_ref, wab_ref, alog_ref, dtb_ref,
                     p_ref, gb_ref, h_ref, pbuf_ref, *, tm, seq, ctx, lat_tiles, n_qk, n_conv, heads):
    i = pl.program_id(0)
    j = pl.program_id(1)
    latent = i < lat_tiles

    @pl.when(j == 0)
    def _():
        first = jnp.logical_and(latent, (i * tm) % seq == 0)
        final = jnp.logical_and(latent, ((i + 1) * tm) % seq == 0)
        hp = _pre(xp_ref[...], mod_ref, gpre_ref)
        hn = _pre(xn_ref[...], mod_ref, gpre_ref)
        h_ref[0:HALO, :] = jnp.where(first, 0.0, hp).astype(BF16)
        hm = _pre(x_ref[...], mod_ref, gpre_ref).astype(BF16)
        h_ref[HALO:HALO + tm, :] = hm
        h_ref[HALO + tm:, :] = jnp.where(final, 0.0, hn).astype(BF16)
        ab = _dot(hm, wab_ref[...])
        lane = lax.broadcasted_iota(jnp.int32, ab.shape, 1)
        z = ab + dtb_ref[...]
        softplus = jnp.maximum(z, 0.0) + jnp.log1p(jnp.exp(-jnp.abs(z)))
        logdecay = -jnp.exp(alog_ref[...]) * softplus
        gb_ref[...] = jnp.where(lane < 2 * heads, logdecay, jax.nn.sigmoid(ab))

    p = _dot(h_ref[...], w_ref[...])

    @pl.when(j >= n_conv)
    def _():
        p_ref[...] = p[HALO:HALO + tm, :]

    def finish(acc):
        z = _silu(acc)

        @pl.when(j >= n_qk)
        def _():
            p_ref[...] = z

        @pl.when(j < n_qk)
        def _():
            scale = jnp.where(j < n_qk // 2, GDN_DK ** -0.5, 1.0)
            for c in range(z.shape[1] // GDN_DK):
                zc = z[:, c * GDN_DK:(c + 1) * GDN_DK]
                inv = lax.rsqrt(jnp.sum(zc * zc, axis=-1, keepdims=True) + NORM_EPS) * scale
                p_ref[:, c * GDN_DK:(c + 1) * GDN_DK] = zc * inv

    def taps():
        for t in range(GDN_CONV):
            off = t - GDN_CONV // 2
            yield off, pbuf_ref[HALO + off:HALO + off + tm, :], cw_ref[t:t + 1, :]

    @pl.when(jnp.logical_and(j < n_conv, latent))
    def _():
        pbuf_ref[...] = p
        acc = None
        for _, tap, cw in taps():
            acc = tap * cw if acc is None else acc + tap * cw
        finish(acc)

    @pl.when(jnp.logical_and(j < n_conv, jnp.logical_not(latent)))
    def _():
        pbuf_ref[...] = p
        r = (i * tm) % ctx + lax.broadcasted_iota(jnp.int32, (tm, 1), 0)
        wraps = jnp.floor((r.astype(F32) + 0.5) / float(ctx)).astype(jnp.int32)
        pos = r - wraps * ctx
        acc = None
        for off, tap, cw in taps():
            ok = jnp.logical_and(pos + off >= 0, pos + off < ctx)
            term = jnp.where(ok, tap, 0.0) * cw
            acc = term if acc is None else acc + term
        finish(acc)


def _gdn_proj(geom, x, mod, gpre, w_main, conv_w, w_ab, alog_row, dtb_row, heads):
    d, tm = geom.d, geom.tm
    n = w_main.shape[1]
    tn = _pick_tile(d, PROJ_TILE, GDN_DK)
    nj = n // tn
    n_conv = 3 * d // tn
    n_qk = 2 * d // tn
    hb = tm // HALO
    last_hb = geom.rows // HALO - 1
    kern = functools.partial(_gdn_proj_kernel, tm=tm, seq=geom.seq, ctx=geom.ctx, lat_tiles=geom.lat_tiles,
                             n_qk=n_qk, n_conv=n_conv, heads=heads)
    return pl.pallas_call(
        kern,
        grid=(geom.all_tiles, nj),
        in_specs=[
            pl.BlockSpec((tm, d), lambda i, j: (i, 0)),
            pl.BlockSpec((HALO, d), lambda i, j: (jnp.maximum(i * hb - 1, 0), 0)),
            pl.BlockSpec((HALO, d), lambda i, j: (jnp.minimum((i + 1) * hb, last_hb), 0)),
            pl.BlockSpec((None, 3, d), lambda i, j: (geom.mod_row(i), 0, 0)),
            pl.BlockSpec((1, d), lambda i, j: (0, 0)),
            pl.BlockSpec((d, tn), lambda i, j: (0, j)),
            pl.BlockSpec((GDN_CONV, tn), lambda i, j: (0, jnp.minimum(j, n_conv - 1))),
            pl.BlockSpec((d, LANES), lambda i, j: (0, 0)),
            pl.BlockSpec((1, LANES), lambda i, j: (0, 0)),
            pl.BlockSpec((1, LANES), lambda i, j: (0, 0)),
        ],
        out_specs=[
            pl.BlockSpec((tm, tn), lambda i, j: (i, j)),
            pl.BlockSpec((tm, LANES), lambda i, j: (i, 0)),
        ],
        out_shape=[
            jax.ShapeDtypeStruct((geom.rows, n), F32),
            jax.ShapeDtypeStruct((geom.rows, LANES), F32),
        ],
        scratch_shapes=[pltpu.VMEM((tm + 2 * HALO, d), BF16), pltpu.VMEM((tm + 2 * HALO, tn), F32)],
        compiler_params=_params(("parallel", "arbitrary")),
        name="gdn_proj",
    )(x, x, x, mod, gpre, w_main, conv_w, w_ab, alog_row, dtb_row)


def _gdn_prep(chains, h, heads):
    n = len(chains)
    qs, ks, vs, gbs, revs = zip(*chains)
    c = qs[0].shape[0]
    row = lax.broadcasted_iota(jnp.int32, (c, c), 0)
    col = lax.broadcasted_iota(jnp.int32, (c, c), 1)
    lane = lax.broadcasted_iota(jnp.int32, gbs[0].shape, 1)
    eye = jnp.where(row == col, 1.0, 0.0)
    incl = {False: row >= col, True: row <= col}
    strict = {False: row > col, True: row < col}
    tri = {r: jnp.where(m, 1.0, 0.0).astype(BF16) for r, m in incl.items()}
    first_lane = jnp.where(lax.broadcasted_iota(jnp.int32, (SUBLANES, LANES), 1) == 0, 1.0, 0.0).astype(BF16)
    rng = range(n)

    def pick(gb, idx):
        colv = jnp.sum(jnp.where(lane == idx, gb, 0.0), axis=1, keepdims=True)
        return jnp.broadcast_to(colv, gb.shape)

    g = [pick(gbs[i], int(revs[i]) * heads + h) for i in rng]
    beta = [pick(gbs[i], 2 * heads + int(revs[i]) * heads + h) for i in rng]

    gp = [_split3(x) for x in g]
    gc = [sum(_dot(tri[revs[i]], p) for p in gp[i]) for i in rng]
    cp = [_split3(x) for x in gc]
    grow = [sum(_dot_nt(first_lane, p) for p in cp[i]) for i in rng]
    gam = []
    for i in rng:
        gr = jnp.broadcast_to(grow[i][0:1, :], (c, c))
        gcol = jnp.concatenate([gc[i]] * (c // LANES), axis=1)
        m = incl[revs[i]]
        gam.append(jnp.where(m, jnp.exp(jnp.where(m, gcol - gr, 0.0)), 0.0))

    kb = [ks[i] * beta[i] for i in rng]
    k16 = [x.astype(BF16) for x in ks]
    kk = [_dot_nt(kb[i].astype(BF16), k16[i]) for i in rng]
    qk = [_dot_nt(qs[i].astype(BF16), k16[i]) for i in rng]
    a = [jnp.where(strict[revs[i]], kk[i] * gam[i], 0.0) for i in rng]
    intra = [(qk[i] * gam[i]).astype(BF16) for i in rng]
    eg = [jnp.exp(x) for x in gc]

    leaf = SUBLANES.bit_length() - 1
    in_leaf = (row >> leaf) == (col >> leaf)
    a8 = [jnp.where(in_leaf, x, 0.0) for x in a]
    a8b = [x.astype(BF16) for x in a8]
    sq = [_dot(x, x).astype(BF16) for x in a8b]
    t = [eye - x for x in a8]
    t = [t[i] + _dot(t[i].astype(BF16), sq[i]) for i in rng]
    qd = [_dot(x, x).astype(BF16) for x in sq]
    t = [t[i] + _dot(t[i].astype(BF16), qd[i]) for i in rng]
    for lvl in range(leaf, c.bit_length() - 1):
        rb = row >> lvl
        cb = col >> lvl
        off = {False: jnp.logical_and(rb == cb + 1, (rb & 1) == 1),
               True: jnp.logical_and(cb == rb + 1, (rb & 1) == 0)}
        t16 = [x.astype(BF16) for x in t]
        inner = [_dot(jnp.where(off[revs[i]], a[i], 0.0).astype(BF16), t16[i]).astype(BF16) for i in rng]
        t = [t[i] - _dot(t16[i], inner[i]) for i in rng]

    r = [jnp.concatenate([vs[i] * beta[i], kb[i] * eg[i]], axis=1) for i in rng]
    r = [r[i] + _dot((t[i] - eye).astype(BF16), r[i].astype(BF16)) for i in rng]

    out = []
    for i in rng:
        g_last = gc[i][0:1, :] if revs[i] else gc[i][c - 1:c, :]
        k_tail = ks[i] * jnp.exp(g_last - gc[i])
        out.append((r[i][:, :GDN_DK], r[i][:, GDN_DK:].astype(BF16), (qs[i] * eg[i]).astype(BF16),
                    k_tail.astype(BF16), intra[i], jnp.exp(g_last)))
    return out


def _gdn_step(u, w, q_dec, k_tail, intra, decay, s):
    s16 = s.astype(BF16)
    vn16 = (u - _dot(w, s16)).astype(BF16)
    o = _dot(q_dec, s16) + _dot(intra, vn16)
    return o, s * decay + _dot_tn(k_tail, vn16)


def _gdn_scan_kernel(ql, kl, vl, gl, gbl, qc, kc, vc, gc, gbc, og_ref, yl_ref, yc_ref,
                     q_s, k_s, v_s, gb_s, u_s, w_s, qd_s, kt_s, in_s, dec_s, o_s, *, heads, chunk, group):
    h = pl.program_id(1)
    n_ctx = qc.shape[0]
    n_tok = q_s.shape[0]
    cc = n_ctx // chunk
    nc = n_tok // chunk

    for dst, c_ref, l_ref in ((q_s, qc, ql), (k_s, kc, kl), (v_s, vc, vl), (gb_s, gbc, gbl)):
        dst[0:n_ctx, :] = c_ref[...]
        dst[n_ctx:, :] = l_ref[...]

    def chunk_at(n, d):
        return n if d == 0 else jnp.where(n < cc, cc - 1 - n, nc - 1 - (n - cc))

    def prep(it):
        chains, where = [], []
        for j in range(group):
            for d in range(2):
                ci = chunk_at(it * group + j, d)
                rows = pl.ds(pl.multiple_of(ci * chunk, chunk), chunk)
                chains.append((q_s[rows, :], k_s[rows, :], v_s[rows, :], gb_s[rows, :], bool(d)))
                where.append((d, rows, pl.ds(pl.multiple_of(ci * SUBLANES, SUBLANES), SUBLANES)))
        for (d, rows, dec_rows), (u, w, qd, kt, intra, dec) in zip(where, _gdn_prep(chains, h, heads)):
            u_s[d, rows, :] = u
            w_s[d, rows, :] = w
            qd_s[d, rows, :] = qd
            kt_s[d, rows, :] = kt
            in_s[d, rows, :] = intra
            dec_s[d, dec_rows, :] = jnp.broadcast_to(dec, (SUBLANES, GDN_DK))

    def scan(it, states):
        states = list(states)
        for j in range(group):
            for d in range(2):
                ci = chunk_at(it * group + j, d)
                rows = pl.ds(pl.multiple_of(ci * chunk, chunk), chunk)
                dec = dec_s[d, pl.ds(pl.multiple_of(ci * SUBLANES, SUBLANES), SUBLANES), :][0:1, :]
                o, states[d] = _gdn_step(u_s[d, rows, :], w_s[d, rows, :], qd_s[d, rows, :], kt_s[d, rows, :],
                                         in_s[d, rows, :], dec, states[d])
                o_s[d, rows, :] = o
        return tuple(states)

    def body(it, states):
        states = scan(it - 1, states)
        prep(it)
        return states

    zero = jnp.zeros((GDN_DK, GDN_DK), F32)
    prep(0)
    states = lax.fori_loop(1, nc // group, body, (zero, zero))
    scan(nc // group - 1, states)

    og = og_ref[...]
    yc_ref[...] = (_rms(o_s[0, 0:n_ctx, :] + o_s[1, 0:n_ctx, :], og) * _silu(gc[...])).astype(yc_ref.dtype)
    yl_ref[...] = (_rms(o_s[0, n_ctx:, :] + o_s[1, n_ctx:, :], og) * _silu(gl[...])).astype(yl_ref.dtype)


def _gdn_scan(geom, p, gb, o_gain, heads):
    b, seq, ctx, d = geom.batch, geom.seq, geom.ctx, geom.d
    chunk = _pick_tile(math.gcd(seq, ctx), GDN_CHUNK, LANES)
    n_tok = ctx + seq
    nc = n_tok // chunk
    group = max(g for g in range(1, GDN_PREP_GROUP + 1) if nc % g == 0)
    cb = geom.n_lat // ctx

    def lat(part):
        return pl.BlockSpec((seq, GDN_DK), lambda bi, hi: (bi, part * heads + hi))

    def cx(part):
        return pl.BlockSpec((ctx, GDN_DK), lambda bi, hi: (cb + bi, part * heads + hi))

    in_specs = [lat(0), lat(1), lat(2), lat(3), pl.BlockSpec((seq, LANES), lambda bi, hi: (bi, 0)),
                cx(0), cx(1), cx(2), cx(3), pl.BlockSpec((ctx, LANES), lambda bi, hi: (cb + bi, 0)),
                pl.BlockSpec((1, GDN_DK), lambda bi, hi: (0, 0))]
    tok = lambda dt: pltpu.VMEM((n_tok, GDN_DK), dt)
    per_dir = lambda cols, dt: pltpu.VMEM((2, n_tok, cols), dt)
    return pl.pallas_call(
        functools.partial(_gdn_scan_kernel, heads=heads, chunk=chunk, group=group),
        grid=(b, heads),
        in_specs=in_specs,
        out_specs=[pl.BlockSpec((seq, GDN_DK), lambda bi, hi: (bi, hi)),
                   pl.BlockSpec((ctx, GDN_DK), lambda bi, hi: (bi, hi))],
        out_shape=[jax.ShapeDtypeStruct((geom.n_lat, d), BF16), jax.ShapeDtypeStruct((geom.n_ctx, d), BF16)],
        scratch_shapes=[tok(F32), tok(F32), tok(F32), tok(F32),
                        per_dir(GDN_DK, F32), per_dir(GDN_DK, BF16), per_dir(GDN_DK, BF16),
                        per_dir(GDN_DK, BF16), per_dir(chunk, BF16),
                        pltpu.VMEM((2, nc * SUBLANES, GDN_DK), F32), per_dir(GDN_DK, F32)],
        compiler_params=_params(("parallel", "parallel")),
        name="gdn_scan",
    )(p, p, p, p, gb, p, p, p, p, gb, o_gain)


PAIR = 2 * LANES
HALF_HD = DIFF_HD // 2


def _pair_layout_columns(d):
    idx = []
    for p in range(d // PAIR):
        for half in range(2):
            for hh in range(2):
                for m in range(2):
                    for i in range(HALF_HD):
                        idx.append((2 * p + hh) * LANES + m * DIFF_HD + 2 * i + half)
    return jnp.asarray(idx, jnp.int32)


def _diff_proj_kernel(x_ref, mod_ref, gpre_ref, w_ref, cos_ref, sin_ref, o_ref, h_ref, *, lat_tiles, n_q, n_qk):
    i = pl.program_id(0)
    j = pl.program_id(1)

    @pl.when(j == 0)
    def _():
        h_ref[...] = _pre(x_ref[...], mod_ref, gpre_ref).astype(BF16)

    p = _dot(h_ref[...], w_ref[...])
    scale = jnp.where(j < n_q, DIFF_HD ** -0.5 * math.log2(math.e), 1.0)
    rope = jnp.logical_and(i < lat_tiles, j < n_qk)

    @pl.when(rope)
    def _():
        cos = cos_ref[...]
        sin = sin_ref[...]
        for b in range(p.shape[1] // PAIR):
            x1 = p[:, b * PAIR:b * PAIR + LANES]
            x2 = p[:, b * PAIR + LANES:(b + 1) * PAIR]
            o_ref[:, b * PAIR:b * PAIR + LANES] = ((x1 * cos - x2 * sin) * scale).astype(o_ref.dtype)
            o_ref[:, b * PAIR + LANES:(b + 1) * PAIR] = ((x1 * sin + x2 * cos) * scale).astype(o_ref.dtype)

    @pl.when(jnp.logical_not(rope))
    def _():
        o_ref[...] = (p * scale).astype(o_ref.dtype)


def _diff_proj(geom, x, mod, gpre, w, cos_t, sin_t):
    d, tm = geom.d, geom.tm
    n = w.shape[1]
    tn = _pick_tile(d, PROJ_TILE, PAIR)
    per_seq = geom.seq // tm
    kern = functools.partial(_diff_proj_kernel, lat_tiles=geom.lat_tiles, n_q=d // tn, n_qk=2 * d // tn)
    return pl.pallas_call(
        kern,
        grid=(geom.all_tiles, n // tn),
        in_specs=[
            pl.BlockSpec((tm, d), lambda i, j: (i, 0)),
            pl.BlockSpec((None, 3, d), lambda i, j: (geom.mod_row(i), 0, 0)),
            pl.BlockSpec((1, d), lambda i, j: (0, 0)),
            pl.BlockSpec((d, tn), lambda i, j: (0, j)),
            pl.BlockSpec((tm, LANES), lambda i, j: (i % per_seq, 0)),
            pl.BlockSpec((tm, LANES), lambda i, j: (i % per_seq, 0)),
        ],
        out_specs=pl.BlockSpec((tm, tn), lambda i, j: (i, j)),
        out_shape=jax.ShapeDtypeStruct((geom.rows, n), BF16),
        scratch_shapes=[pltpu.VMEM((tm, d), BF16)],
        compiler_params=_params(("parallel", "arbitrary")),
        name="diff_proj",
    )(x, mod, gpre, w, cos_t, sin_t)


def _diff_attn_kernel(*refs, n_seg, lam_init):
    q_ref = refs[0]
    k_refs = refs[1:1 + n_seg]
    v_refs = refs[1 + n_seg:1 + 2 * n_seg]
    lam_ref, subln_ref, o_ref = refs[1 + 2 * n_seg:]

    lp = lam_ref[...]
    lam = (jnp.exp(jnp.sum(lp[0:1, :] * lp[1:2, :], axis=1, keepdims=True))
           - jnp.exp(jnp.sum(lp[2:3, :] * lp[3:4, :], axis=1, keepdims=True)) + lam_init)

    q = q_ref[...]
    group = (lax.broadcasted_iota(jnp.int32, q.shape, 1) >> (HALF_HD.bit_length() - 1)) & 3
    zero = jnp.zeros_like(q)
    subln = subln_ref[...]

    def scores(n):
        qm = jnp.where(group == n, q, zero)
        return [_dot_nt(qm, k_ref[...]) for k_ref in k_refs]

    def weighted(n, s):
        hh = n // 2
        mx = functools.reduce(jnp.maximum, [jnp.max(t, axis=1, keepdims=True) for t in s])
        acc = None
        for t in range(n_seg):
            e = jnp.exp2(s[t] - mx).astype(BF16)
            vaug = jnp.concatenate([v_refs[t][:, hh * LANES:(hh + 1) * LANES],
                                    jnp.ones((v_refs[t].shape[0], LANES), BF16)], axis=1)
            term = _dot(e, vaug)
            acc = term if acc is None else acc + term
        return acc[:, :LANES] * (1.0 / acc[:, LANES:])

    s_next = scores(0)
    num = []
    for n in range(4):
        s_cur = s_next
        if n + 1 < 4:
            s_next = scores(n + 1)
        num.append(weighted(n, s_cur))
    for hh in range(2):
        o = num[2 * hh] - lam * num[2 * hh + 1]
        o_ref[:, hh * LANES:(hh + 1) * LANES] = (_rms(o, subln) * (1.0 - lam_init)).astype(o_ref.dtype)


def _diff_attn(geom, qkv, lam_p, subln, lam_init, context_queries):
    b, seq, ctx, d = geom.batch, geom.seq, geom.ctx, geom.d
    pairs = d // PAIR
    cb = geom.n_lat // ctx

    def lat(part):
        return pl.BlockSpec((seq, PAIR), lambda bi, pi, qi: (bi, part * pairs + pi))

    def cx(part):
        return pl.BlockSpec((ctx, PAIR), lambda bi, pi, qi: (cb + bi, part * pairs + pi))

    if context_queries:
        tq = _pick_tile(ctx, Q_TILE, SUBLANES)
        q0 = geom.n_lat // tq
        nq = ctx // tq
        k_specs, v_specs = [cx(1)], [cx(2)]
        out_rows = geom.n_ctx
    else:
        tq = _pick_tile(seq, Q_TILE, SUBLANES)
        q0 = 0
        nq = seq // tq
        k_specs, v_specs = [cx(1), lat(1)], [cx(2), lat(2)]
        out_rows = geom.n_lat
    n_seg = len(k_specs)
    in_specs = ([pl.BlockSpec((tq, PAIR), lambda bi, pi, qi: (q0 + bi * nq + qi, pi))] + k_specs + v_specs
                + [pl.BlockSpec(lam_p.shape, lambda bi, pi, qi: (0, 0)),
                   pl.BlockSpec((1, LANES), lambda bi, pi, qi: (0, 0))])
    return pl.pallas_call(
        functools.partial(_diff_attn_kernel, n_seg=n_seg, lam_init=lam_init),
        grid=(b, pairs, nq),
        in_specs=in_specs,
        out_specs=pl.BlockSpec((tq, PAIR), lambda bi, pi, qi: (bi * nq + qi, pi)),
        out_shape=jax.ShapeDtypeStruct((out_rows, d), BF16),
        compiler_params=_params(("parallel", "parallel", "arbitrary")),
        name="diff_attn_ctx" if context_queries else "diff_attn",
    )(*([qkv] * (1 + 2 * n_seg)), lam_p, subln)


def _rope_tables(seq):
    inv = ROPE_BASE ** (-jnp.arange(0, HALF_HD, 2, dtype=F32) / HALF_HD)
    pos = jnp.arange(seq)
    row = (pos // GRID_W).astype(F32)
    col = (pos % GRID_W).astype(F32)
    ang = jnp.concatenate([row[:, None] * inv, col[:, None] * inv], axis=-1)
    return jnp.tile(jnp.cos(ang), (1, LANES // HALF_HD)), jnp.tile(jnp.sin(ang), (1, LANES // HALF_HD))


def kernel(x, c, ctx, c_ctx, w_mod, b_mod, norm_pre, norm_post, ffn_w_in, ffn_w_out, gdn_w_in, gdn_conv, gdn_A_log,
           gdn_dt_bias, gdn_o_gain, gdn_w_out, diff_w_in, diff_lambda, diff_subln, diff_w_out):
    batch, seq, d = x.shape
    n_ctx_tok = ctx.shape[1]
    depth = w_mod.shape[0]
    heads = d // GDN_DK
    geom = _Geom(batch, seq, n_ctx_tok, d)

    mod_rows = -(-(batch + 1) // SUBLANES) * SUBLANES
    cvec = jnp.concatenate([c, c_ctx[None], jnp.zeros((mod_rows - batch - 1, d), F32)], axis=0)
    mod = _modulation(cvec, w_mod, b_mod).reshape(depth, mod_rows, N_SUB, 3, d)

    cos_t, sin_t = _rope_tables(seq)
    pair_cols = _pair_layout_columns(d)
    xs = jnp.concatenate([x.reshape(batch * seq, d), ctx.reshape(batch * n_ctx_tok, d)], axis=0)

    for l in range(depth):
        last = l == depth - 1
        i = l // 2
        gpre = norm_pre[l].reshape(N_SUB, 1, d)
        gpost = norm_post[l].reshape(N_SUB, 1, d)
        w_in16 = ffn_w_in[l].astype(BF16)
        w_out16 = ffn_w_out[l].astype(BF16)

        xs = _ffn(geom, xs, mod[l, :, 0], gpre[0], gpost[0], w_in16[0], w_out16[0], geom.all_tiles)

        if l % 2 == 0:
            w = gdn_w_in[i]
            w_ab = jnp.pad(w[:, 4 * d:], ((0, 0), (0, LANES - 4 * heads))).astype(BF16)
            pad = (0, LANES - 2 * heads)
            alog_row = jnp.pad(gdn_A_log[i].reshape(-1), pad).reshape(1, LANES)
            dtb_row = jnp.pad(gdn_dt_bias[i].reshape(-1), pad).reshape(1, LANES)
            p, gb = _gdn_proj(geom, xs, mod[l, :, 1], gpre[1], w[:, :4 * d].astype(BF16), gdn_conv[i], w_ab,
                              alog_row, dtb_row, heads)
            yl, yc = _gdn_scan(geom, p, gb, gdn_o_gain[i].reshape(1, GDN_DK), heads)
            w_o = gdn_w_out[i].astype(BF16)
        else:
            lam_init = 0.8 - 0.6 * math.exp(-0.3 * l)
            w = diff_w_in[i]
            w = jnp.concatenate([jnp.take(w[:, :d], pair_cols, axis=1), jnp.take(w[:, d:2 * d], pair_cols, axis=1),
                                 w[:, 2 * d:]], axis=1).astype(BF16)
            qkv = _diff_proj(geom, xs, mod[l, :, 1], gpre[1], w, cos_t, sin_t)
            subln = diff_subln[i].reshape(1, 2 * DIFF_HD)
            yl = _diff_attn(geom, qkv, diff_lambda[i], subln, lam_init, context_queries=False)
            yc = None if last else _diff_attn(geom, qkv, diff_lambda[i], subln, lam_init, context_queries=True)
            w_o = diff_w_out[i].astype(BF16)

        xs = _outproj(geom, xs, yl, None if last else yc, mod[l, :, 1], gpost[1], w_o)
        xs = _ffn(geom, xs, mod[l, :, 2], gpre[2], gpost[2], w_in16[1], w_out16[1],
                  geom.lat_tiles if last else geom.all_tiles)

    return xs.reshape(batch, seq, d)
```
